```python
import jax, jax.numpy as jnp
from jax import lax
import numpy as np

D_MODEL = 1024
BATCH = 4
SEQ = 4096
DEPTH = 2
DEC_BATCH = 8
DEC_SEQ = 16
PAST_LEN = 2048

CHUNK = 64
N_A_LAYERS = DEPTH // 2
N_B_LAYERS = DEPTH - N_A_LAYERS
D_RNN = D_MODEL
N_RG_BLOCKS = 4
RG_BLOCK = D_RNN // N_RG_BLOCKS
CONV_WIDTH = 4
RG_C = 8.0
N_HEADS = 16
HEAD_DIM = 64
N_KV_HEADS = 2
GROUP = N_HEADS // N_KV_HEADS
WINDOW = 128
WIN_CHUNKS = WINDOW // CHUNK
EPS = 1e-6

kernel_name = "yoco_rglru_swa_sink_stream_step"


def rms_norm(x, g):
    xf = x.astype(jnp.float32)
    y = xf * lax.rsqrt(jnp.mean(xf * xf, axis=-1, keepdims=True) + EPS) * g.astype(jnp.float32)
    return y.astype(x.dtype)


def alibi_slopes():
    return 2.0 ** (-8.0 * jnp.arange(1, N_HEADS + 1, dtype=jnp.float32) / N_HEADS)


def causal_conv(x, hist, w, b):
    T = x.shape[1]
    xe = jnp.concatenate([hist.astype(x.dtype), x], axis=1)
    y = sum(xe[:, k:k + T] * w[k] for k in range(CONV_WIDTH)) + b
    return y, xe[:, -(CONV_WIDTH - 1):]


def rglru(x, h0, wa, ba, wx, bx, lam):
    B, T, C = x.shape
    xf = x.astype(jnp.float32)
    xb = xf.reshape(B, T, N_RG_BLOCKS, RG_BLOCK)
    gr = jnp.einsum('btnc,ncd->btnd', xb, wa.astype(jnp.float32)).reshape(B, T, C) + ba.astype(jnp.float32)
    gi = jnp.einsum('btnc,ncd->btnd', xb, wx.astype(jnp.float32)).reshape(B, T, C) + bx.astype(jnp.float32)
    r = jax.nn.sigmoid(gr)
    i = jax.nn.sigmoid(gi)
    log_a = RG_C * r * jax.nn.log_sigmoid(lam.astype(jnp.float32))
    a = jnp.exp(log_a)
    b = jnp.sqrt(-jnp.expm1(2.0 * log_a)) * (i * xf)
    b = b.at[:, 0].add(a[:, 0] * h0.astype(jnp.float32))

    def comb(lhs, rhs):
        a1, b1 = lhs
        a2, b2 = rhs
        return a1 * a2, a2 * b1 + b2

    _, h = lax.associative_scan(comb, (a, b), axis=1)
    return h.astype(x.dtype), h[:, -1].astype(x.dtype)


def a_layer(x, conv_hist, h0, norm_g, w_in, conv_w, conv_b, wa, ba, wx, bx, lam, w_out):
    u = rms_norm(x, norm_g) @ w_in
    xb, gate = u[..., :D_RNN], u[..., D_RNN:]
    xc, conv_new = causal_conv(xb, conv_hist, conv_w, conv_b)
    h, h_last = rglru(xc, h0, wa, ba, wx, bx, lam)
    y = (h * jax.nn.silu(gate)) @ w_out
    return x + y, conv_new, h_last


def shared_kv(x, kv_norm, w_kv, k_norm):
    B, T, _ = x.shape
    u = rms_norm(x, kv_norm) @ w_kv
    kw = N_KV_HEADS * HEAD_DIM
    k = rms_norm(u[..., :kw].reshape(B, T, N_KV_HEADS, HEAD_DIM), k_norm)
    v = u[..., kw:].reshape(B, T, N_KV_HEADS, HEAD_DIM)
    return k, v


def b_queries(x, norm_g, w_in, q_norm):
    B, T, _ = x.shape
    u = rms_norm(x, norm_g) @ w_in
    qw = N_HEADS * HEAD_DIM
    q = rms_norm(u[..., :qw].reshape(B, T, N_HEADS, HEAD_DIM), q_norm) * (HEAD_DIM ** -0.5)
    return q, u[..., qw:]


def sink_softmax(s, sink):
    m = jnp.maximum(jnp.max(s, axis=-1, keepdims=True), sink)
    p = jnp.exp(s - m)
    return p / (jnp.sum(p, axis=-1, keepdims=True) + jnp.exp(sink - m))


def swa_prompt(q, k, v, sinks):
    B, T, _, _ = q.shape
    NC = T // CHUNK
    KB = (WIN_CHUNKS + 1) * CHUNK
    qc = q.reshape(B, NC, CHUNK, N_KV_HEADS, GROUP, HEAD_DIM)
    pad = ((0, 0), (WIN_CHUNKS, 0), (0, 0), (0, 0), (0, 0))
    kp = jnp.pad(k.reshape(B, NC, CHUNK, N_KV_HEADS, HEAD_DIM), pad)
    vp = jnp.pad(v.reshape(B, NC, CHUNK, N_KV_HEADS, HEAD_DIM), pad)
    kband = jnp.concatenate([kp[:, j:j + NC] for j in range(WIN_CHUNKS + 1)], axis=2)
    vband = jnp.concatenate([vp[:, j:j + NC] for j in range(WIN_CHUNKS + 1)], axis=2)
    s = jnp.einsum('bnqkgd,bnskd->bnkgqs', qc, kband, preferred_element_type=jnp.float32)
    qi = jnp.arange(CHUNK)[:, None]
    sj = jnp.arange(KB)[None, :] - WIN_CHUNKS * CHUNK
    dist = jnp.abs(qi - sj).astype(jnp.float32)
    slopes = alibi_slopes().reshape(N_KV_HEADS, GROUP)[:, :, None, None]
    s = s - slopes * dist
    key_chunk = jnp.arange(NC)[:, None] - WIN_CHUNKS + jnp.arange(KB)[None, :] // CHUNK
    valid = (key_chunk >= 0)[None, :, None, None, None, :]
    s = jnp.where(valid, s, -jnp.inf)
    p = sink_softmax(s, sinks.astype(jnp.float32).reshape(N_KV_HEADS, GROUP)[:, :, None, None])
    o = jnp.einsum('bnkgqs,bnskd->bnqkgd', p, vband.astype(jnp.float32))
    return o.reshape(B, T, N_HEADS * HEAD_DIM).astype(q.dtype)


def swa_sample(q, k_all, v_all, past_len, sinks):
    B, T, _, _ = q.shape
    S = k_all.shape[1]
    qg = q.reshape(B, T, N_KV_HEADS, GROUP, HEAD_DIM)
    s = jnp.einsum('btkgd,bskd->bkgts', qg, k_all, preferred_element_type=jnp.float32)
    dist = jnp.abs(jnp.arange(T)[:, None] + past_len - jnp.arange(S)[None, :]).astype(jnp.float32)
    slopes = alibi_slopes().reshape(N_KV_HEADS, GROUP)[:, :, None, None]
    s = s - slopes * dist
    p = sink_softmax(s, sinks.astype(jnp.float32).reshape(N_KV_HEADS, GROUP)[:, :, None, None])
    o = jnp.einsum('bkgts,bskd->btkgd', p, v_all.astype(jnp.float32))
    return o.reshape(B, T, N_HEADS * HEAD_DIM).astype(q.dtype)


def run_group(x, conv_state, h_state, past_k, past_v, a_p, kv_p, b_p):
    conv_out, h_out = [], []
    k = v = k_buf = v_buf = None
    for layer in range(DEPTH):
        if layer < N_A_LAYERS:
            i = layer
            x, c_new, h_last = a_layer(x, conv_state[i], h_state[i], *[p[i] for p in a_p])
            conv_out.append(c_new)
            h_out.append(h_last)
            if layer == N_A_LAYERS - 1:
                k_new, v_new = shared_kv(x, *kv_p)
                if past_k is None:
                    k, v = k_new, v_new
                    k_buf, v_buf = k_new[:, -WINDOW:], v_new[:, -WINDOW:]
                else:
                    L = past_k.shape[1]
                    k = jnp.concatenate([past_k.astype(x.dtype), k_new], axis=1)
                    v = jnp.concatenate([past_v.astype(x.dtype), v_new], axis=1)
                    k_buf, v_buf = k[:, -L:], v[:, -L:]
        else:
            j = layer - N_A_LAYERS
            b_norm, b_w_in, q_norm, sinks, b_w_out = [p[j] for p in b_p]
            q, gate = b_queries(x, b_norm, b_w_in, q_norm)
            if past_k is None:
                o = swa_prompt(q, k, v, sinks)
            else:
                o = swa_sample(q, k, v, past_k.shape[1], sinks)
            x = x + (o * jax.nn.silu(gate)) @ b_w_out
    return x, k_buf, v_buf, jnp.stack(conv_out, axis=0), jnp.stack(h_out, axis=0)


def setup_inputs(seed: int = 0) -> dict:
    key = jax.random.key(seed)
    ks = iter(jax.random.split(key, 32))
    nrm = lambda shape, scale: jax.random.normal(next(ks), shape, jnp.float32) * scale
    f32 = jnp.float32
    kv_len = min(WINDOW, PAST_LEN)
    u = jax.random.uniform(next(ks), (N_A_LAYERS, D_RNN), f32, 0.9, 0.999)
    a_base = u ** (1.0 / RG_C)
    a_lambda = jnp.log(a_base) - jnp.log1p(-a_base)
    return {
        "x_prompt": nrm((BATCH, SEQ, D_MODEL), 1.0),
        "x_sample": nrm((DEC_BATCH, DEC_SEQ, D_MODEL), 1.0),
        "cache_k": nrm((DEC_BATCH, kv_len, N_KV_HEADS, HEAD_DIM), 1.0),
        "cache_v": nrm((DEC_BATCH, kv_len, N_KV_HEADS, HEAD_DIM), 1.0),
        "state_conv": nrm((N_A_LAYERS, DEC_BATCH, CONV_WIDTH - 1, D_RNN), 1.0),
        "state_rglru": nrm((N_A_LAYERS, DEC_BATCH, D_RNN), 0.5),
        "a_norm": 1.0 + nrm((N_A_LAYERS, D_MODEL), 0.02),
        "a_w_in": nrm((N_A_LAYERS, D_MODEL, 2 * D_RNN), D_MODEL ** -0.5),
        "a_conv_w": nrm((N_A_LAYERS, CONV_WIDTH, D_RNN), CONV_WIDTH ** -0.5),
        "a_conv_b": nrm((N_A_LAYERS, D_RNN), 0.02),
        "a_gate_a_w": nrm((N_A_LAYERS, N_RG_BLOCKS, RG_BLOCK, RG_BLOCK), RG_BLOCK ** -0.5),
        "a_gate_a_b": nrm((N_A_LAYERS, D_RNN), 0.02),
        "a_gate_x_w": nrm((N_A_LAYERS, N_RG_BLOCKS, RG_BLOCK, RG_BLOCK), RG_BLOCK ** -0.5),
        "a_gate_x_b": nrm((N_A_LAYERS, D_RNN), 0.02),
        "a_lambda": a_lambda,
        "a_w_out": nrm((N_A_LAYERS, D_RNN, D_MODEL), D_RNN ** -0.5),
        "kv_norm": 1.0 + nrm((D_MODEL,), 0.02),
        "w_kv": nrm((D_MODEL, 2 * N_KV_HEADS * HEAD_DIM), D_MODEL ** -0.5),
        "k_norm": 1.0 + nrm((HEAD_DIM,), 0.02),
        "b_norm": 1.0 + nrm((N_B_LAYERS, D_MODEL), 0.02),
        "b_w_in": nrm((N_B_LAYERS, D_MODEL, 2 * N_HEADS * HEAD_DIM), D_MODEL ** -0.5),
        "q_norm": 1.0 + nrm((N_B_LAYERS, HEAD_DIM), 0.02),
        "sinks": nrm((N_B_LAYERS, N_HEADS), 0.5),
        "b_w_out": nrm((N_B_LAYERS, N_HEADS * HEAD_DIM, D_MODEL), (N_HEADS * HEAD_DIM) ** -0.5),
    }


def reference(x_prompt, x_sample, cache_k, cache_v, state_conv, state_rglru,
              a_norm, a_w_in, a_conv_w, a_conv_b, a_gate_a_w, a_gate_a_b, a_gate_x_w, a_gate_x_b,
              a_lambda, a_w_out, kv_norm, w_kv, k_norm, b_norm, b_w_in, q_norm, sinks, b_w_out):
    a_p = (a_norm, a_w_in, a_conv_w, a_conv_b, a_gate_a_w, a_gate_a_b, a_gate_x_w, a_gate_x_b, a_lambda, a_w_out)
    kv_p = (kv_norm, w_kv, k_norm)
    b_p = (b_norm, b_w_in, q_norm, sinks, b_w_out)
    Bp = x_prompt.shape[0]
    p_conv0 = jnp.zeros((N_A_LAYERS, Bp, CONV_WIDTH - 1, D_RNN), x_prompt.dtype)
    p_h0 = jnp.zeros((N_A_LAYERS, Bp, D_RNN), x_prompt.dtype)
    y_prompt, p_k, p_v, p_conv, p_h = run_group(x_prompt, p_conv0, p_h0, None, None, a_p, kv_p, b_p)
    y_sample, s_k, s_v, s_conv, s_h = run_group(x_sample, state_conv, state_rglru, cache_k, cache_v, a_p, kv_p, b_p)
    return (y_prompt, y_sample, p_k, p_v, p_conv, p_h, s_k, s_v, s_conv, s_h)
```

```python
import functools

import jax
import jax.numpy as jnp
from jax import lax
from jax.experimental import pallas as pl
from jax.experimental.pallas import tpu as pltpu

F32 = jnp.float32
BF16 = jnp.bfloat16

EPS = 1e-6
CONV_WIDTH = 4
N_RG_BLOCKS = 4
RG_C = 8.0
N_HEADS = 16
HEAD_DIM = 64
N_KV_HEADS = 2
GROUP = N_HEADS // N_KV_HEADS
WINDOW = 128
CHUNK = 64

SUBLANES = 8
LANES = 128
VMEM_LIMIT_BYTES = 48 * 1024 * 1024

PAIRS = GROUP // 2
KEY_SLOTS = 2 * LANES
NEG = -1e30


def _sigmoid(x):
    return jax.nn.sigmoid(x)


def _a_layer_kernel(x_ref, hist_ref, h0_ref, g_ref, win_ref, cw_ref, cb_ref, wa_ref, ba_ref,
                    wx_ref, bx_ref, lam_ref, wout_ref,
                    x1_ref, conv_ref, hlast_ref,
                    xbe_ref, a_ref, b_ref, h_ref, hc_ref, *, tile_t, d, n_t):
    t = pl.program_id(1)
    hdr = SUBLANES

    @pl.when(t == 0)
    def _():
        xbe_ref[0:hdr, :] = jnp.zeros((hdr, d), F32)
        xbe_ref[hdr - (CONV_WIDTH - 1):hdr, :] = hist_ref[0]
        hc_ref[...] = jnp.broadcast_to(h0_ref[0], (SUBLANES, d))

    x = x_ref[0]
    ms = jnp.mean(x * x, axis=-1, keepdims=True)
    xn = (x * lax.rsqrt(ms + EPS) * g_ref[...]).astype(BF16)
    u = jnp.dot(xn, win_ref[...], preferred_element_type=F32)
    xbe_ref[hdr:hdr + tile_t, :] = u[:, :d]
    gate = u[:, d:]

    xc = cb_ref[...]
    for k in range(CONV_WIDTH):
        off = hdr - (CONV_WIDTH - 1) + k
        xc = xc + xbe_ref[off:off + tile_t, :] * cw_ref[k:k + 1, :]

    xcb = xc.astype(BF16)
    blk = d // N_RG_BLOCKS
    grs, gis = [], []
    for n in range(N_RG_BLOCKS):
        xs = xcb[:, n * blk:(n + 1) * blk]
        grs.append(jnp.dot(xs, wa_ref[n], preferred_element_type=F32))
        gis.append(jnp.dot(xs, wx_ref[n], preferred_element_type=F32))
    r = _sigmoid(jnp.concatenate(grs, axis=1) + ba_ref[...])
    i = _sigmoid(jnp.concatenate(gis, axis=1) + bx_ref[...])
    log_a = r * (RG_C * jax.nn.log_sigmoid(lam_ref[...]))
    a = jnp.exp(log_a)
    a_ref[...] = a
    b_ref[...] = jnp.sqrt(-jnp.tanh(log_a) * (a * a + 1.0)) * (i * xc)

    row = lax.broadcasted_iota(jnp.int32, (SUBLANES, d), 0)

    def scan_block(j, hc):
        r0 = pl.multiple_of(j * SUBLANES, SUBLANES)
        a = a_ref[pl.ds(r0, SUBLANES), :]
        b = b_ref[pl.ds(r0, SUBLANES), :]
        for dist in (1, 2, 4):
            a_prev = pltpu.roll(a, dist, 0)
            b_prev = pltpu.roll(b, dist, 0)
            m = row >= dist
            b = jnp.where(m, a * b_prev + b, b)
            a = jnp.where(m, a * a_prev, a)
        h = a * hc + b
        h_ref[pl.ds(r0, SUBLANES), :] = h
        return jnp.broadcast_to(h[SUBLANES - 1:SUBLANES, :], (SUBLANES, d))

    hc = lax.fori_loop(0, tile_t // SUBLANES, scan_block, hc_ref[...])
    hc_ref[...] = hc

    hg = (h_ref[...] * (gate * _sigmoid(gate))).astype(BF16)
    y = jnp.dot(hg, wout_ref[...], preferred_element_type=F32)
    x1_ref[0] = x + y

    xbe_ref[0:hdr, :] = xbe_ref[tile_t:tile_t + hdr, :]

    @pl.when(t == n_t - 1)
    def _():
        conv_ref[0] = xbe_ref[hdr - (CONV_WIDTH - 1):hdr, :]
        hlast_ref[0] = hc_ref[0:1, :]


def _a_layer(x, hist, h0, g, w_in, conv_w, conv_b, wa, ba, wx, bx, lam, w_out, *, tile_t):
    bsz, seq, d = x.shape
    assert seq % tile_t == 0 and tile_t % SUBLANES == 0 and tile_t >= SUBLANES
    n_t = seq // tile_t
    kern = functools.partial(_a_layer_kernel, tile_t=tile_t, d=d, n_t=n_t)
    row = lambda a: a.reshape(1, d)
    full = lambda shape: pl.BlockSpec(shape, lambda b, t: (0,) * len(shape))
    return pl.pallas_call(
        kern,
        grid=(bsz, n_t),
        in_specs=[
            pl.BlockSpec((1, tile_t, d), lambda b, t: (b, t, 0)),
            pl.BlockSpec((1, CONV_WIDTH - 1, d), lambda b, t: (b, 0, 0)),
            pl.BlockSpec((1, 1, d), lambda b, t: (b, 0, 0)),
            full((1, d)),
            full((d, 2 * d)),
            full((CONV_WIDTH, d)),
            full((1, d)),
            full((N_RG_BLOCKS, d // N_RG_BLOCKS, d // N_RG_BLOCKS)),
            full((1, d)),
            full((N_RG_BLOCKS, d // N_RG_BLOCKS, d // N_RG_BLOCKS)),
            full((1, d)),
            full((1, d)),
            full((d, d)),
        ],
        out_specs=[
            pl.BlockSpec((1, tile_t, d), lambda b, t: (b, t, 0)),
            pl.BlockSpec((1, CONV_WIDTH - 1, d), lambda b, t: (b, 0, 0)),
            pl.BlockSpec((1, 1, d), lambda b, t: (b, 0, 0)),
        ],
        out_shape=[
            jax.ShapeDtypeStruct((bsz, seq, d), F32),
            jax.ShapeDtypeStruct((bsz, CONV_WIDTH - 1, d), F32),
            jax.ShapeDtypeStruct((bsz, 1, d), F32),
        ],
        scratch_shapes=[
            pltpu.VMEM((SUBLANES + tile_t, d), F32),
            pltpu.VMEM((tile_t, d), F32),
            pltpu.VMEM((tile_t, d), F32),
            pltpu.VMEM((tile_t, d), F32),
            pltpu.VMEM((SUBLANES, d), F32),
        ],
        compiler_params=pltpu.CompilerParams(
            dimension_semantics=("arbitrary", "arbitrary"),
            vmem_limit_bytes=VMEM_LIMIT_BYTES),
        name="a_layer",
    )(x, hist, h0.reshape(bsz, 1, d), row(g), w_in, conv_w, row(conv_b), wa, row(ba), wx, row(bx),
      row(lam), w_out)


def _b_layer_kernel(sink_ref, x1_ref, pk_ref, pv_ref, kvg_ref, bg_ref, wkv_ref, kn_ref, e128_ref,
                    win_ref, qn_ref, e256_ref, bias_ref, wout_ref,
                    y_ref, ko_ref, vo_ref,
                    kf_ref, vf_ref, kx_ref, vx_ref, o_ref,
                    *, tile_t, chunk, d, n_t, has_past):
    t = pl.program_id(1)
    n_c = tile_t // chunk
    rows_x = kx_ref.shape[1]
    lane = lax.broadcasted_iota(jnp.int32, (1, LANES), 1)
    lo = lane < HEAD_DIM

    def expand_k(kfull):
        rolled = pltpu.roll(kfull, HEAD_DIM, 1)
        zero = jnp.zeros_like(kfull)
        return (jnp.where(lo, kfull, zero), jnp.where(lo, zero, rolled),
                jnp.where(lo, rolled, zero), jnp.where(lo, zero, kfull))

    def store_expanded(r0, nrows, kfull, vfull):
        for j, (kv, vv) in enumerate(zip(expand_k(kfull), expand_k(vfull))):
            kx_ref[j, r0:r0 + nrows, :] = kv.astype(BF16)
            vx_ref[j, r0:r0 + nrows, 0:LANES] = vv.astype(BF16)

    @pl.when(t == 0)
    def _():
        ones_lo = jnp.broadcast_to(jnp.where(lo, 1.0, 0.0), (rows_x, LANES)).astype(BF16)
        ones_hi = jnp.broadcast_to(jnp.where(lo, 0.0, 1.0), (rows_x, LANES)).astype(BF16)
        for j in range(4):
            vx_ref[j, :, LANES:2 * LANES] = ones_lo if j % 2 == 0 else ones_hi
            kx_ref[j, WINDOW:rows_x, :] = jnp.zeros((rows_x - WINDOW, LANES), BF16)
            vx_ref[j, WINDOW:rows_x, 0:LANES] = jnp.zeros((rows_x - WINDOW, LANES), BF16)
        kf_ref[0:WINDOW, :] = pk_ref[0]
        vf_ref[0:WINDOW, :] = pv_ref[0]
        store_expanded(0, WINDOW, pk_ref[0], pv_ref[0])

    x1 = x1_ref[0]
    ms = jnp.mean(x1 * x1, axis=-1, keepdims=True)
    xhat = x1 * lax.rsqrt(ms + EPS)
    xk = (xhat * kvg_ref[...]).astype(BF16)
    xq = (xhat * bg_ref[...]).astype(BF16)

    kv = jnp.dot(xk, wkv_ref[...], preferred_element_type=F32)
    k_raw = kv[:, 0:LANES]
    v_new = kv[:, LANES:2 * LANES]
    k_ms = jnp.dot((k_raw * k_raw).astype(BF16), e128_ref[...], preferred_element_type=F32)
    k_new = k_raw * lax.rsqrt(k_ms + EPS) * kn_ref[...]
    kf_ref[WINDOW:WINDOW + tile_t, :] = k_new
    vf_ref[WINDOW:WINDOW + tile_t, :] = v_new
    store_expanded(WINDOW, tile_t, k_new, v_new)

    u = jnp.dot(xq, win_ref[...], preferred_element_type=F32)
    q_raw = u[:, :d]
    gate = u[:, d:]
    q_sq = (q_raw * q_raw).astype(BF16)
    seg = e256_ref.shape[0]
    q_ms = jnp.concatenate(
        [jnp.dot(q_sq[:, j * seg:(j + 1) * seg], e256_ref[...], preferred_element_type=F32)
         for j in range(d // seg)], axis=1)
    q = (q_raw * lax.rsqrt(q_ms + EPS) * qn_ref[...] * (HEAD_DIM ** -0.5)).astype(BF16)

    col = lax.broadcasted_iota(jnp.int32, (1, 2 * KEY_SLOTS), 1) % KEY_SLOTS
    for c in range(n_c):
        r0 = c * chunk
        for g in range(N_KV_HEADS):
            k_bd = jnp.concatenate([kx_ref[2 * g, r0:r0 + KEY_SLOTS, :],
                                    kx_ref[2 * g + 1, r0:r0 + KEY_SLOTS, :]], axis=0)
            v_bd = jnp.concatenate([vx_ref[2 * g, r0:r0 + KEY_SLOTS, :],
                                    vx_ref[2 * g + 1, r0:r0 + KEY_SLOTS, :]], axis=0)
            q_g = jnp.concatenate(
                [q[r0:r0 + chunk, (g * PAIRS + p) * LANES:(g * PAIRS + p + 1) * LANES]
                 for p in range(PAIRS)], axis=0)
            s = lax.dot_general(q_g, k_bd, (((1,), (1,)), ((), ())),
                                preferred_element_type=F32)
            s = s + bias_ref[g]
            if not has_past and r0 < WINDOW:
                n_dead = jnp.where(t == 0, WINDOW - r0, 0)
                s = s + jnp.where(col < n_dead, NEG, 0.0)
            p_blocks, sink_terms = [], []
            for p in range(PAIRS):
                row_p, st_p = [], []
                for e in range(2):
                    sink = sink_ref[g * GROUP + 2 * p + e]
                    sb = s[p * chunk:(p + 1) * chunk, e * KEY_SLOTS:(e + 1) * KEY_SLOTS]
                    m = jnp.maximum(jnp.max(sb, axis=-1, keepdims=True), sink)
                    row_p.append(jnp.exp(sb - m).astype(BF16))
                    st_p.append(jnp.exp(sink - m))
                p_blocks.append(jnp.concatenate(row_p, axis=1))
                sink_terms.append(jnp.where(lo, st_p[0], st_p[1]))
            prob = jnp.concatenate(p_blocks, axis=0)
            ov = jnp.dot(prob, v_bd, preferred_element_type=F32)
            den = ov[:, LANES:2 * LANES] + jnp.concatenate(sink_terms, axis=0)
            o = ov[:, 0:LANES] / den
            for p in range(PAIRS):
                c0 = (g * PAIRS + p) * LANES
                o_ref[r0:r0 + chunk, c0:c0 + LANES] = o[p * chunk:(p + 1) * chunk, :]

    og = (o_ref[...] * (gate * _sigmoid(gate))).astype(BF16)
    y_ref[0] = x1 + jnp.dot(og, wout_ref[...], preferred_element_type=F32)

    @pl.when(t == n_t - 1)
    def _():
        ko_ref[0] = kf_ref[tile_t:tile_t + WINDOW, :]
        vo_ref[0] = vf_ref[tile_t:tile_t + WINDOW, :]

    if n_t > 1:
        kf_ref[0:WINDOW, :] = kf_ref[tile_t:tile_t + WINDOW, :]
        vf_ref[0:WINDOW, :] = vf_ref[tile_t:tile_t + WINDOW, :]
        for j in range(4):
            kx_ref[j, 0:WINDOW, :] = kx_ref[j, tile_t:tile_t + WINDOW, :]
            vx_ref[j, 0:WINDOW, 0:LANES] = vx_ref[j, tile_t:tile_t + WINDOW, 0:LANES]


def _alibi_bias(chunk):
    slopes = 2.0 ** (-8.0 * jnp.arange(1, N_HEADS + 1, dtype=F32) / N_HEADS)
    slopes = slopes.reshape(N_KV_HEADS, PAIRS, 1, 2, 1)
    qi = jnp.arange(chunk, dtype=F32).reshape(1, 1, chunk, 1, 1)
    sj = jnp.arange(KEY_SLOTS, dtype=F32).reshape(1, 1, 1, 1, KEY_SLOTS)
    dist = jnp.abs(qi + WINDOW - sj)
    bias = jnp.where(sj < WINDOW + chunk, -(slopes * dist), NEG)
    return bias.reshape(N_KV_HEADS, PAIRS * chunk, 2 * KEY_SLOTS)


def _head_mean_matrix(n):
    idx = jnp.arange(n) // HEAD_DIM
    return jnp.where(idx[:, None] == idx[None, :], 1.0 / HEAD_DIM, 0.0).astype(BF16)


def _b_layer(x1, past_k, past_v, kv_g, b_g, w_kv, k_norm, w_in, q_norm, sinks, w_out,
             *, tile_t, chunk, has_past):
    bsz, seq, d = x1.shape
    kvw = N_KV_HEADS * HEAD_DIM
    assert kvw == LANES and d == N_HEADS * HEAD_DIM
    assert seq % tile_t == 0 and tile_t % chunk == 0 and WINDOW + chunk <= KEY_SLOTS
    n_t = seq // tile_t
    assert n_t == 1 or tile_t >= WINDOW
    rows_x = tile_t - chunk + KEY_SLOTS
    kern = functools.partial(_b_layer_kernel, tile_t=tile_t, chunk=chunk, d=d, n_t=n_t,
                             has_past=has_past)
    full = lambda shape: pl.BlockSpec(shape, lambda b, t: (0,) * len(shape))
    seg = 2 * LANES
    return pl.pallas_call(
        kern,
        grid=(bsz, n_t),
        in_specs=[
            pl.BlockSpec(memory_space=pltpu.SMEM),
            pl.BlockSpec((1, tile_t, d), lambda b, t: (b, t, 0)),
            pl.BlockSpec((1, WINDOW, kvw), lambda b, t: (b, 0, 0)),
            pl.BlockSpec((1, WINDOW, kvw), lambda b, t: (b, 0, 0)),
            full((1, d)),
            full((1, d)),
            full((d, 2 * kvw)),
            full((1, kvw)),
            full((kvw, kvw)),
            full((d, 2 * d)),
            full((1, d)),
            full((seg, seg)),
            full((N_KV_HEADS, PAIRS * chunk, 2 * KEY_SLOTS)),
            full((d, d)),
        ],
        out_specs=[
            pl.BlockSpec((1, tile_t, d), lambda b, t: (b, t, 0)),
            pl.BlockSpec((1, WINDOW, kvw), lambda b, t: (b, 0, 0)),
            pl.BlockSpec((1, WINDOW, kvw), lambda b, t: (b, 0, 0)),
        ],
        out_shape=[
            jax.ShapeDtypeStruct((bsz, seq, d), F32),
            jax.ShapeDtypeStruct((bsz, WINDOW, kvw), F32),
            jax.ShapeDtypeStruct((bsz, WINDOW, kvw), F32),
        ],
        scratch_shapes=[
            pltpu.VMEM((WINDOW + tile_t, kvw), F32),
            pltpu.VMEM((WINDOW + tile_t, kvw), F32),
            pltpu.VMEM((4, rows_x, LANES), BF16),
            pltpu.VMEM((4, rows_x, 2 * LANES), BF16),
            pltpu.VMEM((tile_t, d), F32),
        ],
        compiler_params=pltpu.CompilerParams(
            dimension_semantics=("arbitrary", "arbitrary"),
            vmem_limit_bytes=VMEM_LIMIT_BYTES),
        name="b_layer",
    )(sinks, x1, past_k, past_v, kv_g.reshape(1, d), b_g.reshape(1, d), w_kv,
      jnp.tile(k_norm, N_KV_HEADS).reshape(1, kvw), _head_mean_matrix(kvw), w_in,
      jnp.tile(q_norm, N_HEADS).reshape(1, d), _head_mean_matrix(seg), _alibi_bias(chunk), w_out)


def kernel(x_prompt, x_sample, cache_k, cache_v, state_conv, state_rglru, a_norm, a_w_in, a_conv_w,
           a_conv_b, a_gate_a_w, a_gate_a_b, a_gate_x_w, a_gate_x_b, a_lambda, a_w_out, kv_norm, w_kv,
           k_norm, b_norm, b_w_in, q_norm, sinks, b_w_out):
    assert a_norm.shape[0] == 1 and b_norm.shape[0] == 1
    d = x_prompt.shape[-1]
    kvw = N_KV_HEADS * HEAD_DIM
    a_w = (a_norm[0], a_w_in[0].astype(BF16), a_conv_w[0], a_conv_b[0], a_gate_a_w[0].astype(BF16),
           a_gate_a_b[0], a_gate_x_w[0].astype(BF16), a_gate_x_b[0], a_lambda[0],
           a_w_out[0].astype(BF16))
    b_w = (kv_norm, b_norm[0], w_kv.astype(BF16), k_norm, b_w_in[0].astype(BF16), q_norm[0], sinks[0],
           b_w_out[0].astype(BF16))

    def run_group(x, conv_state, h_state, past_k, past_v, has_past, tile_t, chunk):
        bsz = x.shape[0]
        x1, conv_new, h_last = _a_layer(x, conv_state, h_state, *a_w, tile_t=tile_t)
        y, k_buf, v_buf = _b_layer(x1, past_k.reshape(bsz, WINDOW, kvw), past_v.reshape(bsz, WINDOW, kvw),
                                   *b_w, tile_t=tile_t, chunk=chunk, has_past=has_past)
        shape_kv = (bsz, WINDOW, N_KV_HEADS, HEAD_DIM)
        return (y, k_buf.reshape(shape_kv), v_buf.reshape(shape_kv), conv_new[None],
                h_last.reshape(bsz, d)[None])

    bp = x_prompt.shape[0]
    zeros_kv = jnp.zeros((bp, WINDOW, kvw), F32)
    y_p, p_k, p_v, p_conv, p_h = run_group(
        x_prompt, jnp.zeros((bp, CONV_WIDTH - 1, d), F32), jnp.zeros((bp, d), F32),
        zeros_kv, zeros_kv, False, 256, CHUNK)
    y_s, s_k, s_v, s_conv, s_h = run_group(
        x_sample, state_conv[0], state_rglru[0], cache_k, cache_v, True,
        x_sample.shape[1], x_sample.shape[1])
    return (y_p, y_s, p_k, p_v, p_conv, p_h, s_k, s_v, s_conv, s_h)
```

```python
import functools

import jax
import jax.numpy as jnp
from jax import lax
from jax.experimental import pallas as pl
from jax.experimental.pallas import tpu as pltpu

F32 = jnp.float32
BF16 = jnp.bfloat16

EPS = 1e-6
CONV_WIDTH = 4
N_RG_BLOCKS = 4
RG_C = 8.0
N_HEADS = 16
HEAD_DIM = 64
N_KV_HEADS = 2
GROUP = N_HEADS // N_KV_HEADS
WINDOW = 128
CHUNK = 64

SUBLANES = 8
LANES = 128
VMEM_LIMIT_BYTES = 48 * 1024 * 1024

PAIRS = GROUP // 2
KEY_SLOTS = 2 * LANES
NEG = -1e30


def _sigmoid(x):
    return jax.nn.sigmoid(x)


def _segment_permutation(rows):
    steps = rows // SUBLANES
    r = jnp.arange(rows)
    src = (r % SUBLANES) * steps + r // SUBLANES
    return (src[:, None] == jnp.arange(rows)[None, :]).astype(BF16)


def _a_layer_kernel(x_ref, hist_ref, h0_ref, pm_ref, pmt_ref, g_ref, win_ref, cw_ref, cb_ref,
                    wa_ref, ba_ref, wx_ref, bx_ref, lam_ref, wout_ref,
                    x1_ref, tail_ref, hlast_ref,
                    xbe_ref, prev_ref, hc_ref, *, rows, d, chained):
    t = pl.program_id(1)
    steps = rows // SUBLANES
    taps = CONV_WIDTH - 1
    hdr = taps * SUBLANES
    sub = lax.broadcasted_iota(jnp.int32, (SUBLANES, d), 0)

    if chained:
        @pl.when(t == 0)
        def _():
            prev_ref[...] = hist_ref[...]
            hc_ref[...] = h0_ref[...]

    x = x_ref[0]
    ms = jnp.mean(x * x, axis=-1, keepdims=True)
    xn = (x * lax.rsqrt(ms + EPS) * g_ref[...]).astype(BF16)
    xn = jnp.dot(pm_ref[...], xn, preferred_element_type=F32).astype(BF16)
    u = jnp.dot(xn, win_ref[...], preferred_element_type=F32)
    xb = u[:, :d]
    gate = u[:, d:]
    xbe_ref[hdr:hdr + rows, :] = xb

    if chained:
        for i in range(taps):
            r0 = (steps - taps + i) * SUBLANES
            before = pltpu.roll(prev_ref[i * SUBLANES:(i + 1) * SUBLANES, :], 1, 0)
            within = pltpu.roll(xb[r0:r0 + SUBLANES, :], 1, 0)
            xbe_ref[i * SUBLANES:(i + 1) * SUBLANES, :] = jnp.where(sub == 0, before, within)
        prev_ref[...] = xb[rows - hdr:rows, :]
    else:
        xbe_ref[0:hdr, :] = hist_ref[...]
    tail_ref[0] = xb[rows - hdr:rows, :]

    xc = cb_ref[...]
    for k in range(CONV_WIDTH):
        off = k * SUBLANES
        xc = xc + xbe_ref[off:off + rows, :] * cw_ref[k:k + 1, :]

    xcb = xc.astype(BF16)
    blk = d // N_RG_BLOCKS
    grs, gis = [], []
    for n in range(N_RG_BLOCKS):
        xs = xcb[:, n * blk:(n + 1) * blk]
        grs.append(jnp.dot(xs, wa_ref[n], preferred_element_type=F32))
        gis.append(jnp.dot(xs, wx_ref[n], preferred_element_type=F32))
    r = _sigmoid(jnp.concatenate(grs, axis=1) + ba_ref[...])
    i = _sigmoid(jnp.concatenate(gis, axis=1) + bx_ref[...])
    log_a = r * (RG_C * jax.nn.log_sigmoid(lam_ref[...]))
    a = jnp.exp(log_a)
    one_minus_a2 = -jnp.tanh(log_a) * (a * a + 1.0)
    root = jnp.where(one_minus_a2 > 0.0, one_minus_a2 * lax.rsqrt(one_minus_a2), 0.0)
    b = root * (i * xc)

    h = b[0:SUBLANES, :]
    c = a[0:SUBLANES, :]
    h_loc, a_cum = [h], [c]
    for p in range(1, steps):
        a_p = a[p * SUBLANES:(p + 1) * SUBLANES, :]
        h = a_p * h + b[p * SUBLANES:(p + 1) * SUBLANES, :]
        c = c * a_p
        h_loc.append(h)
        a_cum.append(c)

    if chained:
        a_seg, b_seg = c, h
        for dist in (1, 2, 4):
            a_prev = pltpu.roll(a_seg, dist, 0)
            b_prev = pltpu.roll(b_seg, dist, 0)
            m = sub >= dist
            b_seg = jnp.where(m, a_seg * b_prev + b_seg, b_seg)
            a_seg = jnp.where(m, a_seg * a_prev, a_seg)
        after = a_seg * hc_ref[...] + b_seg
        h_in = jnp.where(sub == 0, hc_ref[...], pltpu.roll(after, 1, 0))
        h_end = jnp.broadcast_to(after[SUBLANES - 1:SUBLANES, :], (SUBLANES, d))
        hc_ref[...] = h_end
    else:
        h_in = h0_ref[...]
    h_all = [h_loc[p] + a_cum[p] * h_in for p in range(steps)]
    hlast_ref[0] = h_end if chained else h_all[steps - 1]

    hg = (jnp.concatenate(h_all, axis=0) * (gate * _sigmoid(gate))).astype(BF16)
    hg = jnp.dot(pmt_ref[...], hg, preferred_element_type=F32).astype(BF16)
    x1_ref[0] = x + jnp.dot(hg, wout_ref[...], preferred_element_type=F32)


def _a_layer(x, hist, h0, g, w_in, conv_w, conv_b, wa, ba, wx, bx, lam, w_out, *, rows, chained):
    n_seq, n_rows, d = x.shape
    taps = CONV_WIDTH - 1
    assert n_rows % rows == 0 and rows % SUBLANES == 0 and rows // SUBLANES >= taps
    n_t = n_rows // rows
    assert chained or n_t == 1
    kern = functools.partial(_a_layer_kernel, rows=rows, d=d, chained=chained)
    row = lambda a: a.reshape(1, d)
    full = lambda shape: pl.BlockSpec(shape, lambda b, t: (0,) * len(shape))
    perm = _segment_permutation(rows)
    nb = N_RG_BLOCKS
    return pl.pallas_call(
        kern,
        grid=(n_seq, n_t),
        in_specs=[
            pl.BlockSpec((1, rows, d), lambda b, t: (b, t, 0)),
            full((taps * SUBLANES, d)),
            full((SUBLANES, d)),
            full((rows, rows)),
            full((rows, rows)),
            full((1, d)),
            full((d, 2 * d)),
            full((CONV_WIDTH, d)),
            full((1, d)),
            full((nb, d // nb, d // nb)),
            full((1, d)),
            full((nb, d // nb, d // nb)),
            full((1, d)),
            full((1, d)),
            full((d, d)),
        ],
        out_specs=[
            pl.BlockSpec((1, rows, d), lambda b, t: (b, t, 0)),
            pl.BlockSpec((1, taps * SUBLANES, d), lambda b, t: (b, 0, 0)),
            pl.BlockSpec((1, SUBLANES, d), lambda b, t: (b, 0, 0)),
        ],
        out_shape=[
            jax.ShapeDtypeStruct((n_seq, n_rows, d), F32),
            jax.ShapeDtypeStruct((n_seq, taps * SUBLANES, d), F32),
            jax.ShapeDtypeStruct((n_seq, SUBLANES, d), F32),
        ],
        scratch_shapes=[
            pltpu.VMEM((taps * SUBLANES + rows, d), F32),
            pltpu.VMEM((taps * SUBLANES, d), F32),
            pltpu.VMEM((SUBLANES, d), F32),
        ],
        compiler_params=pltpu.CompilerParams(
            dimension_semantics=("arbitrary", "arbitrary"),
            vmem_limit_bytes=VMEM_LIMIT_BYTES),
        name="a_layer",
    )(x, hist, h0, perm, perm.T, row(g), w_in, conv_w, row(conv_b), wa, row(ba), wx, row(bx),
      row(lam), w_out)


def _b_layer_kernel(sink_ref, x1_ref, pk_ref, pv_ref, kvg_ref, bg_ref, wkv_ref, kn_ref, e128_ref,
                    win_ref, qn_ref, e256_ref, bias_ref, wout_ref,
                    y_ref, ko_ref, vo_ref,
                    kf_ref, vf_ref, kx_ref, vx_ref, o_ref,
                    *, tile_t, chunk, d, n_t, has_past):
    t = pl.program_id(1)
    n_c = tile_t // chunk
    rows_x = kx_ref.shape[1]
    lane = lax.broadcasted_iota(jnp.int32, (1, LANES), 1)
    lo = lane < HEAD_DIM

    def expand_k(kfull):
        rolled = pltpu.roll(kfull, HEAD_DIM, 1)
        zero = jnp.zeros_like(kfull)
        return (jnp.where(lo, kfull, zero), jnp.where(lo, zero, rolled),
                jnp.where(lo, rolled, zero), jnp.where(lo, zero, kfull))

    def store_expanded(r0, nrows, kfull, vfull):
        for j, (kv, vv) in enumerate(zip(expand_k(kfull), expand_k(vfull))):
            kx_ref[j, r0:r0 + nrows, :] = kv.astype(BF16)
            vx_ref[j, r0:r0 + nrows, 0:LANES] = vv.astype(BF16)

    @pl.when(t == 0)
    def _():
        ones_lo = jnp.broadcast_to(jnp.where(lo, 1.0, 0.0), (rows_x, LANES)).astype(BF16)
        ones_hi = jnp.broadcast_to(jnp.where(lo, 0.0, 1.0), (rows_x, LANES)).astype(BF16)
        for j in range(4):
            vx_ref[j, :, LANES:2 * LANES] = ones_lo if j % 2 == 0 else ones_hi
            kx_ref[j, WINDOW:rows_x, :] = jnp.zeros((rows_x - WINDOW, LANES), BF16)
            vx_ref[j, WINDOW:rows_x, 0:LANES] = jnp.zeros((rows_x - WINDOW, LANES), BF16)
        kf_ref[0:WINDOW, :] = pk_ref[0]
        vf_ref[0:WINDOW, :] = pv_ref[0]
        store_expanded(0, WINDOW, pk_ref[0], pv_ref[0])

    x1 = x1_ref[0]
    ms = jnp.mean(x1 * x1, axis=-1, keepdims=True)
    xhat = x1 * lax.rsqrt(ms + EPS)
    xk = (xhat * kvg_ref[...]).astype(BF16)
    xq = (xhat * bg_ref[...]).astype(BF16)

    kv = jnp.dot(xk, wkv_ref[...], preferred_element_type=F32)
    k_raw = kv[:, 0:LANES]
    v_new = kv[:, LANES:2 * LANES]
    k_ms = jnp.dot((k_raw * k_raw).astype(BF16), e128_ref[...], preferred_element_type=F32)
    k_new = k_raw * lax.rsqrt(k_ms + EPS) * kn_ref[...]
    kf_ref[WINDOW:WINDOW + tile_t, :] = k_new
    vf_ref[WINDOW:WINDOW + tile_t, :] = v_new
    store_expanded(WINDOW, tile_t, k_new, v_new)

    u = jnp.dot(xq, win_ref[...], preferred_element_type=F32)
    q_raw = u[:, :d]
    gate = u[:, d:]
    q_sq = (q_raw * q_raw).astype(BF16)
    seg = e256_ref.shape[0]
    q_ms = jnp.concatenate(
        [jnp.dot(q_sq[:, j * seg:(j + 1) * seg], e256_ref[...], preferred_element_type=F32)
         for j in range(d // seg)], axis=1)
    q = (q_raw * lax.rsqrt(q_ms + EPS) * qn_ref[...] * (HEAD_DIM ** -0.5)).astype(BF16)

    col = lax.broadcasted_iota(jnp.int32, (1, 2 * KEY_SLOTS), 1) % KEY_SLOTS
    for c in range(n_c):
        r0 = c * chunk
        for g in range(N_KV_HEADS):
            k_bd = jnp.concatenate([kx_ref[2 * g, r0:r0 + KEY_SLOTS, :],
                                    kx_ref[2 * g + 1, r0:r0 + KEY_SLOTS, :]], axis=0)
            v_bd = jnp.concatenate([vx_ref[2 * g, r0:r0 + KEY_SLOTS, :],
                                    vx_ref[2 * g + 1, r0:r0 + KEY_SLOTS, :]], axis=0)
            q_g = jnp.concatenate(
                [q[r0:r0 + chunk, (g * PAIRS + p) * LANES:(g * PAIRS + p + 1) * LANES]
                 for p in range(PAIRS)], axis=0)
            s = lax.dot_general(q_g, k_bd, (((1,), (1,)), ((), ())),
                                preferred_element_type=F32)
            s = s + bias_ref[g]
            if not has_past and r0 < WINDOW:
                n_dead = jnp.where(t == 0, WINDOW - r0, 0)
                s = s + jnp.where(col < n_dead, NEG, 0.0)
            p_blocks, sink_terms = [], []
            for p in range(PAIRS):
                row_p, st_p = [], []
                for e in range(2):
                    sink = sink_ref[g * GROUP + 2 * p + e]
                    sb = s[p * chunk:(p + 1) * chunk, e * KEY_SLOTS:(e + 1) * KEY_SLOTS]
                    m = jnp.maximum(jnp.max(sb, axis=-1, keepdims=True), sink)
                    row_p.append(jnp.exp(sb - m).astype(BF16))
                    st_p.append(jnp.exp(sink - m))
                p_blocks.append(jnp.concatenate(row_p, axis=1))
                sink_terms.append(jnp.where(lo, st_p[0], st_p[1]))
            prob = jnp.concatenate(p_blocks, axis=0)
            ov = jnp.dot(prob, v_bd, preferred_element_type=F32)
            den = ov[:, LANES:2 * LANES] + jnp.concatenate(sink_terms, axis=0)
            o = ov[:, 0:LANES] / den
            for p in range(PAIRS):
                c0 = (g * PAIRS + p) * LANES
                o_ref[r0:r0 + chunk, c0:c0 + LANES] = o[p * chunk:(p + 1) * chunk, :]

    og = (o_ref[...] * (gate * _sigmoid(gate))).astype(BF16)
    y_ref[0] = x1 + jnp.dot(og, wout_ref[...], preferred_element_type=F32)

    @pl.when(t == n_t - 1)
    def _():
        ko_ref[0] = kf_ref[tile_t:tile_t + WINDOW, :]
        vo_ref[0] = vf_ref[tile_t:tile_t + WINDOW, :]

    if n_t > 1:
        kf_ref[0:WINDOW, :] = kf_ref[tile_t:tile_t + WINDOW, :]
        vf_ref[0:WINDOW, :] = vf_ref[tile_t:tile_t + WINDOW, :]
        for j in range(4):
            kx_ref[j, 0:WINDOW, :] = kx_ref[j, tile_t:tile_t + WINDOW, :]
            vx_ref[j, 0:WINDOW, 0:LANES] = vx_ref[j, tile_t:tile_t + WINDOW, 0:LANES]


def _alibi_bias(chunk):
    slopes = 2.0 ** (-8.0 * jnp.arange(1, N_HEADS + 1, dtype=F32) / N_HEADS)
    slopes = slopes.reshape(N_KV_HEADS, PAIRS, 1, 2, 1)
    qi = jnp.arange(chunk, dtype=F32).reshape(1, 1, chunk, 1, 1)
    sj = jnp.arange(KEY_SLOTS, dtype=F32).reshape(1, 1, 1, 1, KEY_SLOTS)
    dist = jnp.abs(qi + WINDOW - sj)
    bias = jnp.where(sj < WINDOW + chunk, -(slopes * dist), NEG)
    return bias.reshape(N_KV_HEADS, PAIRS * chunk, 2 * KEY_SLOTS)


def _head_mean_matrix(n):
    idx = jnp.arange(n) // HEAD_DIM
    return jnp.where(idx[:, None] == idx[None, :], 1.0 / HEAD_DIM, 0.0).astype(BF16)


def _b_layer(x1, past_k, past_v, kv_g, b_g, w_kv, k_norm, w_in, q_norm, sinks, w_out,
             *, tile_t, chunk, has_past):
    bsz, seq, d = x1.shape
    kvw = N_KV_HEADS * HEAD_DIM
    assert kvw == LANES and d == N_HEADS * HEAD_DIM
    assert seq % tile_t == 0 and tile_t % chunk == 0 and WINDOW + chunk <= KEY_SLOTS
    n_t = seq // tile_t
    assert n_t == 1 or tile_t >= WINDOW
    rows_x = tile_t - chunk + KEY_SLOTS
    kern = functools.partial(_b_layer_kernel, tile_t=tile_t, chunk=chunk, d=d, n_t=n_t,
                             has_past=has_past)
    full = lambda shape: pl.BlockSpec(shape, lambda b, t: (0,) * len(shape))
    seg = 2 * LANES
    return pl.pallas_call(
        kern,
        grid=(bsz, n_t),
        in_specs=[
            pl.BlockSpec(memory_space=pltpu.SMEM),
            pl.BlockSpec((1, tile_t, d), lambda b, t: (b, t, 0)),
            pl.BlockSpec((1, WINDOW, kvw), lambda b, t: (b, 0, 0)),
            pl.BlockSpec((1, WINDOW, kvw), lambda b, t: (b, 0, 0)),
            full((1, d)),
            full((1, d)),
            full((d, 2 * kvw)),
            full((1, kvw)),
            full((kvw, kvw)),
            full((d, 2 * d)),
            full((1, d)),
            full((seg, seg)),
            full((N_KV_HEADS, PAIRS * chunk, 2 * KEY_SLOTS)),
            full((d, d)),
        ],
        out_specs=[
            pl.BlockSpec((1, tile_t, d), lambda b, t: (b, t, 0)),
            pl.BlockSpec((1, WINDOW, kvw), lambda b, t: (b, 0, 0)),
            pl.BlockSpec((1, WINDOW, kvw), lambda b, t: (b, 0, 0)),
        ],
        out_shape=[
            jax.ShapeDtypeStruct((bsz, seq, d), F32),
            jax.ShapeDtypeStruct((bsz, WINDOW, kvw), F32),
            jax.ShapeDtypeStruct((bsz, WINDOW, kvw), F32),
        ],
        scratch_shapes=[
            pltpu.VMEM((WINDOW + tile_t, kvw), F32),
            pltpu.VMEM((WINDOW + tile_t, kvw), F32),
            pltpu.VMEM((4, rows_x, LANES), BF16),
            pltpu.VMEM((4, rows_x, 2 * LANES), BF16),
            pltpu.VMEM((tile_t, d), F32),
        ],
        compiler_params=pltpu.CompilerParams(
            dimension_semantics=("arbitrary", "arbitrary"),
            vmem_limit_bytes=VMEM_LIMIT_BYTES),
        name="b_layer",
    )(sinks, x1, past_k, past_v, kv_g.reshape(1, d), b_g.reshape(1, d), w_kv,
      jnp.tile(k_norm, N_KV_HEADS).reshape(1, kvw), _head_mean_matrix(kvw), w_in,
      jnp.tile(q_norm, N_HEADS).reshape(1, d), _head_mean_matrix(seg), _alibi_bias(chunk), w_out)


def kernel(x_prompt, x_sample, cache_k, cache_v, state_conv, state_rglru, a_norm, a_w_in, a_conv_w,
           a_conv_b, a_gate_a_w, a_gate_a_b, a_gate_x_w, a_gate_x_b, a_lambda, a_w_out, kv_norm, w_kv,
           k_norm, b_norm, b_w_in, q_norm, sinks, b_w_out):
    assert a_norm.shape[0] == 1 and b_norm.shape[0] == 1
    d = x_prompt.shape[-1]
    kvw = N_KV_HEADS * HEAD_DIM
    taps = CONV_WIDTH - 1
    a_w = (a_norm[0], a_w_in[0].astype(BF16), a_conv_w[0], a_conv_b[0], a_gate_a_w[0].astype(BF16),
           a_gate_a_b[0], a_gate_x_w[0].astype(BF16), a_gate_x_b[0], a_lambda[0],
           a_w_out[0].astype(BF16))
    b_w = (kv_norm, b_norm[0], w_kv.astype(BF16), k_norm, b_w_in[0].astype(BF16), q_norm[0], sinks[0],
           b_w_out[0].astype(BF16))

    def b_group(x1, past_k, past_v, has_past, tile_t, chunk):
        bsz = x1.shape[0]
        y, k_buf, v_buf = _b_layer(x1, past_k.reshape(bsz, WINDOW, kvw), past_v.reshape(bsz, WINDOW, kvw),
                                   *b_w, tile_t=tile_t, chunk=chunk, has_past=has_past)
        shape_kv = (bsz, WINDOW, N_KV_HEADS, HEAD_DIM)
        return y, k_buf.reshape(shape_kv), v_buf.reshape(shape_kv)

    bp = x_prompt.shape[0]
    x1_p, tail_p, hl_p = _a_layer(x_prompt, jnp.zeros((taps * SUBLANES, d), F32),
                                  jnp.zeros((SUBLANES, d), F32), *a_w, rows=256, chained=True)
    p_conv = tail_p[:, SUBLANES - 1::SUBLANES, :][None]
    p_h = hl_p[:, 0, :][None]
    zeros_kv = jnp.zeros((bp, WINDOW, kvw), F32)
    y_p, p_k, p_v = b_group(x1_p, zeros_kv, zeros_kv, False, 256, CHUNK)

    bs, ts, _ = x_sample.shape
    assert bs == SUBLANES
    hist_s = jnp.transpose(state_conv[0], (1, 0, 2)).reshape(taps * bs, d)
    x1_s, tail_s, hl_s = _a_layer(x_sample.reshape(1, bs * ts, d), hist_s, state_rglru[0], *a_w,
                                  rows=bs * ts, chained=False)
    s_conv = jnp.transpose(tail_s.reshape(taps, bs, d), (1, 0, 2))[None]
    s_h = hl_s
    y_s, s_k, s_v = b_group(x1_s.reshape(bs, ts, d), cache_k, cache_v, True, ts, ts)
    return (y_p, y_s, p_k, p_v, p_conv, p_h, s_k, s_v, s_conv, s_h)
```

```python
import functools

import jax
import jax.numpy as jnp
from jax import lax
from jax.experimental import pallas as pl
from jax.experimental.pallas import tpu as pltpu

F32 = jnp.float32
BF16 = jnp.bfloat16

EPS = 1e-6
CONV_WIDTH = 4
N_RG_BLOCKS = 4
RG_C = 8.0
N_HEADS = 16
HEAD_DIM = 64
N_KV_HEADS = 2
GROUP = N_HEADS // N_KV_HEADS
WINDOW = 128
CHUNK = 64

SUBLANES = 8
LANES = 128
VMEM_LIMIT_BYTES = 56 * 1024 * 1024

PAIRS = GROUP // 2
KEY_SLOTS = 2 * LANES
NEG = -1e30
A_SEQ_LAG = 2600


def _sigmoid(x):
    return jax.nn.sigmoid(x)


def _pack_rows(w):
    w = w.astype(BF16)
    *lead, k2, n = w.shape
    w = jnp.swapaxes(w.reshape(*lead, k2 // 2, 2, n), -1, -2)
    return lax.bitcast_convert_type(w, jnp.uint32)


def _bf16(packed):
    return pltpu.bitcast(packed, BF16)


def _u32(x_bf16):
    return pltpu.bitcast(x_bf16, jnp.uint32)


def _segment_permutation(rows):
    steps = rows // SUBLANES
    r = jnp.arange(rows)
    src = (r % SUBLANES) * steps + r // SUBLANES
    return (src[:, None] == jnp.arange(rows)[None, :]).astype(BF16)


def _interleave(gens, lag):
    clock = {k: k * lag for k in range(len(gens))}
    while clock:
        k = min(clock, key=lambda j: (clock[j], j))
        try:
            clock[k] += next(gens[k])
        except StopIteration:
            del clock[k]


def _a_layer_kernel(x_ref, hist_ref, h0_ref, pm_ref, pmt_ref, g_ref, win_ref, cw_ref, cb_ref,
                    wa_ref, ba_ref, wx_ref, bx_ref, lam_ref, wout_ref,
                    x1_ref, tail_ref, hlast_ref,
                    xbe_ref, prev_ref, hc_ref, *, rows, d, chained):
    gens = [_a_layer_tile(x_ref.at[n], hist_ref.at[n], h0_ref.at[n], pm_ref, pmt_ref, g_ref, win_ref,
                          cw_ref, cb_ref, wa_ref, ba_ref, wx_ref, bx_ref, lam_ref, wout_ref,
                          x1_ref.at[n], tail_ref.at[n], hlast_ref.at[n],
                          xbe_ref.at[n], prev_ref.at[n], hc_ref.at[n], rows=rows, d=d, chained=chained)
            for n in range(x_ref.shape[0])]
    _interleave(gens, lag=A_SEQ_LAG)


def _a_layer_tile(x_ref, hist_ref, h0_ref, pm_ref, pmt_ref, g_ref, win_ref, cw_ref, cb_ref,
                  wa_ref, ba_ref, wx_ref, bx_ref, lam_ref, wout_ref,
                  x1_ref, tail_ref, hlast_ref,
                  xbe_ref, prev_ref, hc_ref, *, rows, d, chained):
    t = pl.program_id(1)
    steps = rows // SUBLANES
    taps = CONV_WIDTH - 1
    hdr = taps * SUBLANES
    blk = d // N_RG_BLOCKS
    sub = lax.broadcasted_iota(jnp.int32, (SUBLANES, d), 0)

    if chained:
        @pl.when(t == 0)
        def _():
            prev_ref[...] = hist_ref[...]
            hc_ref[...] = h0_ref[...]

    x = x_ref[...]
    ms = jnp.mean(x * x, axis=-1, keepdims=True)
    xn = (x * lax.rsqrt(ms + EPS) * g_ref[...]).astype(BF16)
    xn = jnp.dot(_bf16(pm_ref[...]), xn, preferred_element_type=F32).astype(BF16)
    yield 600

    for n in range(N_RG_BLOCKS):
        xbe_ref[hdr:hdr + rows, n * blk:(n + 1) * blk] = jnp.dot(
            xn, _bf16(win_ref[:, n * blk:(n + 1) * blk]), preferred_element_type=F32)
        yield 256
    gate = []
    for n in range(N_RG_BLOCKS):
        gate.append(jnp.dot(xn, _bf16(win_ref[:, d + n * blk:d + (n + 1) * blk]),
                            preferred_element_type=F32))
        yield 256

    if chained:
        for i in range(taps):
            r0 = hdr + (steps - taps + i) * SUBLANES
            before = pltpu.roll(prev_ref[i * SUBLANES:(i + 1) * SUBLANES, :], 1, 0)
            within = pltpu.roll(xbe_ref[r0:r0 + SUBLANES, :], 1, 0)
            xbe_ref[i * SUBLANES:(i + 1) * SUBLANES, :] = jnp.where(sub == 0, before, within)
        prev_ref[...] = xbe_ref[rows:rows + hdr, :]
    else:
        xbe_ref[0:hdr, :] = hist_ref[...]
    tail_ref[...] = xbe_ref[rows:rows + hdr, :]

    log_decay = RG_C * jax.nn.log_sigmoid(lam_ref[...])
    a_blk, b_blk = [], []
    for n in range(N_RG_BLOCKS):
        cs = slice(n * blk, (n + 1) * blk)
        xc = cb_ref[:, cs]
        for k in range(CONV_WIDTH):
            off = k * SUBLANES
            xc = xc + xbe_ref[off:off + rows, cs] * cw_ref[k:k + 1, cs]
        xcb = xc.astype(BF16)
        r = _sigmoid(jnp.dot(xcb, _bf16(wa_ref[n]), preferred_element_type=F32) + ba_ref[:, cs])
        i = _sigmoid(jnp.dot(xcb, _bf16(wx_ref[n]), preferred_element_type=F32) + bx_ref[:, cs])
        log_a = r * log_decay[:, cs]
        a = jnp.exp(log_a)
        one_minus_a2 = -jnp.tanh(log_a) * (a * a + 1.0)
        root = jnp.where(one_minus_a2 > 0.0, one_minus_a2 * lax.rsqrt(one_minus_a2), 0.0)
        a_blk.append(a)
        b_blk.append(root * (i * xc))
        yield 550
    a = jnp.concatenate(a_blk, axis=1)
    b = jnp.concatenate(b_blk, axis=1)

    h = b[0:SUBLANES, :]
    c = a[0:SUBLANES, :]
    h_loc, a_cum = [h], [c]
    for p in range(1, steps):
        a_p = a[p * SUBLANES:(p + 1) * SUBLANES, :]
        h = a_p * h + b[p * SUBLANES:(p + 1) * SUBLANES, :]
        c = c * a_p
        h_loc.append(h)
        a_cum.append(c)

    if chained:
        a_seg, b_seg = c, h
        for dist in (1, 2, 4):
            a_prev = pltpu.roll(a_seg, dist, 0)
            b_prev = pltpu.roll(b_seg, dist, 0)
            m = sub >= dist
            b_seg = jnp.where(m, a_seg * b_prev + b_seg, b_seg)
            a_seg = jnp.where(m, a_seg * a_prev, a_seg)
        after = a_seg * hc_ref[...] + b_seg
        h_in = jnp.where(sub == 0, hc_ref[...], pltpu.roll(after, 1, 0))
        h_end = jnp.broadcast_to(after[SUBLANES - 1:SUBLANES, :], (SUBLANES, d))
        hc_ref[...] = h_end
    else:
        h_in = h0_ref[...]
    h_all = [h_loc[p] + a_cum[p] * h_in for p in range(steps)]
    hlast_ref[...] = h_end if chained else h_all[steps - 1]
    yield 450

    gate = jnp.concatenate(gate, axis=1)
    hg = (jnp.concatenate(h_all, axis=0) * (gate * _sigmoid(gate))).astype(BF16)
    hg = jnp.dot(_bf16(pmt_ref[...]), hg, preferred_element_type=F32).astype(BF16)
    yield 450
    for n in range(N_RG_BLOCKS):
        cs = slice(n * blk, (n + 1) * blk)
        x1_ref[:, cs] = x[:, cs] + jnp.dot(hg, _bf16(wout_ref[:, cs]), preferred_element_type=F32)
        yield 256


def _a_layer(x, hist, h0, g, w_in, conv_w, conv_b, wa, ba, wx, bx, lam, w_out, *, rows, chained,
             seqs_per_step):
    n_seq, n_rows, d = x.shape
    taps = CONV_WIDTH - 1
    assert n_rows % rows == 0 and rows % SUBLANES == 0 and rows // SUBLANES >= taps
    assert n_seq % seqs_per_step == 0
    n_t = n_rows // rows
    ns = seqs_per_step
    assert chained or n_t == 1
    kern = functools.partial(_a_layer_kernel, rows=rows, d=d, chained=chained)
    row = lambda a: a.reshape(1, d)
    full = lambda shape: pl.BlockSpec(shape, lambda b, t: (0,) * len(shape))
    perm = _segment_permutation(rows)
    nb = N_RG_BLOCKS
    return pl.pallas_call(
        kern,
        grid=(n_seq // ns, n_t),
        in_specs=[
            pl.BlockSpec((ns, rows, d), lambda b, t: (b, t, 0)),
            pl.BlockSpec((ns, taps * SUBLANES, d), lambda b, t: (b, 0, 0)),
            pl.BlockSpec((ns, SUBLANES, d), lambda b, t: (b, 0, 0)),
            full((rows // 2, rows)),
            full((rows // 2, rows)),
            full((1, d)),
            full((d // 2, 2 * d)),
            full((CONV_WIDTH, d)),
            full((1, d)),
            full((nb, d // nb // 2, d // nb)),
            full((1, d)),
            full((nb, d // nb // 2, d // nb)),
            full((1, d)),
            full((1, d)),
            full((d // 2, d)),
        ],
        out_specs=[
            pl.BlockSpec((ns, rows, d), lambda b, t: (b, t, 0)),
            pl.BlockSpec((ns, taps * SUBLANES, d), lambda b, t: (b, 0, 0)),
            pl.BlockSpec((ns, SUBLANES, d), lambda b, t: (b, 0, 0)),
        ],
        out_shape=[
            jax.ShapeDtypeStruct((n_seq, n_rows, d), F32),
            jax.ShapeDtypeStruct((n_seq, taps * SUBLANES, d), F32),
            jax.ShapeDtypeStruct((n_seq, SUBLANES, d), F32),
        ],
        scratch_shapes=[
            pltpu.VMEM((ns, taps * SUBLANES + rows, d), F32),
            pltpu.VMEM((ns, taps * SUBLANES, d), F32),
            pltpu.VMEM((ns, SUBLANES, d), F32),
        ],
        compiler_params=pltpu.CompilerParams(
            dimension_semantics=("arbitrary", "arbitrary"),
            vmem_limit_bytes=VMEM_LIMIT_BYTES),
        name="a_layer",
    )(x, hist, h0, _pack_rows(perm), _pack_rows(perm.T), row(g), w_in, conv_w, row(conv_b), wa, row(ba),
      wx, row(bx), row(lam), w_out)


def _b_layer_kernel(sink_ref, x1_ref, pk_ref, pv_ref, kvg_ref, bg_ref, wkv_ref, kn_ref, e128_ref,
                    win_ref, qn_ref, e256_ref, bias_ref, wout_ref,
                    y_ref, ko_ref, vo_ref,
                    kf_ref, vf_ref, kx_ref, vx_ref, o_ref,
                    *, tile_t, chunk, d, n_t, has_past):
    t = pl.program_id(1)
    n_c = tile_t // chunk
    rows_x = 2 * kx_ref.shape[1]
    lane = lax.broadcasted_iota(jnp.int32, (1, LANES), 1)
    lo = lane < HEAD_DIM

    def expand_k(kfull):
        rolled = pltpu.roll(kfull, HEAD_DIM, 1)
        zero = jnp.zeros_like(kfull)
        return (jnp.where(lo, kfull, zero), jnp.where(lo, zero, rolled),
                jnp.where(lo, rolled, zero), jnp.where(lo, zero, kfull))

    def store_expanded(r0, nrows, kfull, vfull):
        for j, (kv, vv) in enumerate(zip(expand_k(kfull), expand_k(vfull))):
            kx_ref[j, r0 // 2:(r0 + nrows) // 2, :] = _u32(kv.astype(BF16))
            vx_ref[j, r0 // 2:(r0 + nrows) // 2, 0:LANES] = _u32(vv.astype(BF16))

    @pl.when(t == 0)
    def _():
        ones_lo = jnp.broadcast_to(jnp.where(lo, 1.0, 0.0), (rows_x, LANES)).astype(BF16)
        ones_hi = jnp.broadcast_to(jnp.where(lo, 0.0, 1.0), (rows_x, LANES)).astype(BF16)
        for j in range(4):
            vx_ref[j, :, LANES:2 * LANES] = _u32(ones_lo if j % 2 == 0 else ones_hi)
            kx_ref[j, WINDOW // 2:rows_x // 2, :] = jnp.zeros(((rows_x - WINDOW) // 2, LANES), jnp.uint32)
            vx_ref[j, WINDOW // 2:rows_x // 2, 0:LANES] = jnp.zeros(((rows_x - WINDOW) // 2, LANES), jnp.uint32)
        kf_ref[0:WINDOW, :] = pk_ref[0]
        vf_ref[0:WINDOW, :] = pv_ref[0]
        store_expanded(0, WINDOW, pk_ref[0], pv_ref[0])

    x1 = x1_ref[0]
    ms = jnp.mean(x1 * x1, axis=-1, keepdims=True)
    xhat = x1 * lax.rsqrt(ms + EPS)
    xk = (xhat * kvg_ref[...]).astype(BF16)
    xq = (xhat * bg_ref[...]).astype(BF16)

    kv = jnp.dot(xk, _bf16(wkv_ref[...]), preferred_element_type=F32)
    k_raw = kv[:, 0:LANES]
    v_new = kv[:, LANES:2 * LANES]
    k_ms = jnp.dot((k_raw * k_raw).astype(BF16), _bf16(e128_ref[...]), preferred_element_type=F32)
    k_new = k_raw * lax.rsqrt(k_ms + EPS) * kn_ref[...]
    kf_ref[WINDOW:WINDOW + tile_t, :] = k_new
    vf_ref[WINDOW:WINDOW + tile_t, :] = v_new
    store_expanded(WINDOW, tile_t, k_new, v_new)

    u = jnp.dot(xq, _bf16(win_ref[...]), preferred_element_type=F32)
    q_raw = u[:, :d]
    gate = u[:, d:]
    q_sq = (q_raw * q_raw).astype(BF16)
    seg = e256_ref.shape[1]
    q_ms = jnp.concatenate(
        [jnp.dot(q_sq[:, j * seg:(j + 1) * seg], _bf16(e256_ref[...]), preferred_element_type=F32)
         for j in range(d // seg)], axis=1)
    q = (q_raw * lax.rsqrt(q_ms + EPS) * qn_ref[...] * (HEAD_DIM ** -0.5)).astype(BF16)

    col = lax.broadcasted_iota(jnp.int32, (1, 2 * KEY_SLOTS), 1) % KEY_SLOTS
    for c in range(n_c):
        r0 = c * chunk
        for g in range(N_KV_HEADS):
            band = slice(r0 // 2, (r0 + KEY_SLOTS) // 2)
            k_bd = _bf16(jnp.concatenate([kx_ref[2 * g, band, :], kx_ref[2 * g + 1, band, :]], axis=0))
            v_bd = _bf16(jnp.concatenate([vx_ref[2 * g, band, :], vx_ref[2 * g + 1, band, :]], axis=0))
            q_g = jnp.concatenate(
                [q[r0:r0 + chunk, (g * PAIRS + p) * LANES:(g * PAIRS + p + 1) * LANES]
                 for p in range(PAIRS)], axis=0)
            s = lax.dot_general(q_g, k_bd, (((1,), (1,)), ((), ())),
                                preferred_element_type=F32)
            s = s + bias_ref[g]
            if not has_past and r0 < WINDOW:
                n_dead = jnp.where(t == 0, WINDOW - r0, 0)
                s = s + jnp.where(col < n_dead, NEG, 0.0)
            p_blocks, sink_terms = [], []
            for p in range(PAIRS):
                row_p, st_p = [], []
                for e in range(2):
                    sink = sink_ref[g * GROUP + 2 * p + e]
                    sb = s[p * chunk:(p + 1) * chunk, e * KEY_SLOTS:(e + 1) * KEY_SLOTS]
                    m = jnp.maximum(jnp.max(sb, axis=-1, keepdims=True), sink)
                    row_p.append(jnp.exp(sb - m).astype(BF16))
                    st_p.append(jnp.exp(sink - m))
                p_blocks.append(jnp.concatenate(row_p, axis=1))
                sink_terms.append(jnp.where(lo, st_p[0], st_p[1]))
            prob = jnp.concatenate(p_blocks, axis=0)
            ov = jnp.dot(prob, v_bd, preferred_element_type=F32)
            den = ov[:, LANES:2 * LANES] + jnp.concatenate(sink_terms, axis=0)
            o = ov[:, 0:LANES] / den
            for p in range(PAIRS):
                c0 = (g * PAIRS + p) * LANES
                o_ref[r0:r0 + chunk, c0:c0 + LANES] = o[p * chunk:(p + 1) * chunk, :]

    og = (o_ref[...] * (gate * _sigmoid(gate))).astype(BF16)
    y_ref[0] = x1 + jnp.dot(og, _bf16(wout_ref[...]), preferred_element_type=F32)

    @pl.when(t == n_t - 1)
    def _():
        ko_ref[0] = kf_ref[tile_t:tile_t + WINDOW, :]
        vo_ref[0] = vf_ref[tile_t:tile_t + WINDOW, :]

    if n_t > 1:
        kf_ref[0:WINDOW, :] = kf_ref[tile_t:tile_t + WINDOW, :]
        vf_ref[0:WINDOW, :] = vf_ref[tile_t:tile_t + WINDOW, :]
        for j in range(4):
            kx_ref[j, 0:WINDOW // 2, :] = kx_ref[j, tile_t // 2:(tile_t + WINDOW) // 2, :]
            vx_ref[j, 0:WINDOW // 2, 0:LANES] = vx_ref[j, tile_t // 2:(tile_t + WINDOW) // 2, 0:LANES]


def _alibi_bias(chunk):
    slopes = 2.0 ** (-8.0 * jnp.arange(1, N_HEADS + 1, dtype=F32) / N_HEADS)
    slopes = slopes.reshape(N_KV_HEADS, PAIRS, 1, 2, 1)
    qi = jnp.arange(chunk, dtype=F32).reshape(1, 1, chunk, 1, 1)
    sj = jnp.arange(KEY_SLOTS, dtype=F32).reshape(1, 1, 1, 1, KEY_SLOTS)
    dist = jnp.abs(qi + WINDOW - sj)
    bias = jnp.where(sj < WINDOW + chunk, -(slopes * dist), NEG)
    return bias.reshape(N_KV_HEADS, PAIRS * chunk, 2 * KEY_SLOTS)


def _head_mean_matrix(n):
    idx = jnp.arange(n) // HEAD_DIM
    return jnp.where(idx[:, None] == idx[None, :], 1.0 / HEAD_DIM, 0.0).astype(BF16)


def _b_layer(x1, past_k, past_v, kv_g, b_g, w_kv, k_norm, w_in, q_norm, sinks, w_out,
             *, tile_t, chunk, has_past):
    bsz, seq, d = x1.shape
    kvw = N_KV_HEADS * HEAD_DIM
    assert kvw == LANES and d == N_HEADS * HEAD_DIM
    assert seq % tile_t == 0 and tile_t % chunk == 0 and WINDOW + chunk <= KEY_SLOTS
    n_t = seq // tile_t
    assert n_t == 1 or tile_t >= WINDOW
    rows_x = tile_t - chunk + KEY_SLOTS
    kern = functools.partial(_b_layer_kernel, tile_t=tile_t, chunk=chunk, d=d, n_t=n_t,
                             has_past=has_past)
    full = lambda shape: pl.BlockSpec(shape, lambda b, t: (0,) * len(shape))
    seg = 2 * LANES
    return pl.pallas_call(
        kern,
        grid=(bsz, n_t),
        in_specs=[
            pl.BlockSpec(memory_space=pltpu.SMEM),
            pl.BlockSpec((1, tile_t, d), lambda b, t: (b, t, 0)),
            pl.BlockSpec((1, WINDOW, kvw), lambda b, t: (b, 0, 0)),
            pl.BlockSpec((1, WINDOW, kvw), lambda b, t: (b, 0, 0)),
            full((1, d)),
            full((1, d)),
            full((d // 2, 2 * kvw)),
            full((1, kvw)),
            full((kvw // 2, kvw)),
            full((d // 2, 2 * d)),
            full((1, d)),
            full((seg // 2, seg)),
            full((N_KV_HEADS, PAIRS * chunk, 2 * KEY_SLOTS)),
            full((d // 2, d)),
        ],
        out_specs=[
            pl.BlockSpec((1, tile_t, d), lambda b, t: (b, t, 0)),
            pl.BlockSpec((1, WINDOW, kvw), lambda b, t: (b, 0, 0)),
            pl.BlockSpec((1, WINDOW, kvw), lambda b, t: (b, 0, 0)),
        ],
        out_shape=[
            jax.ShapeDtypeStruct((bsz, seq, d), F32),
            jax.ShapeDtypeStruct((bsz, WINDOW, kvw), F32),
            jax.ShapeDtypeStruct((bsz, WINDOW, kvw), F32),
        ],
        scratch_shapes=[
            pltpu.VMEM((WINDOW + tile_t, kvw), F32),
            pltpu.VMEM((WINDOW + tile_t, kvw), F32),
            pltpu.VMEM((4, rows_x // 2, LANES), jnp.uint32),
            pltpu.VMEM((4, rows_x // 2, 2 * LANES), jnp.uint32),
            pltpu.VMEM((tile_t, d), F32),
        ],
        compiler_params=pltpu.CompilerParams(
            dimension_semantics=("arbitrary", "arbitrary"),
            vmem_limit_bytes=VMEM_LIMIT_BYTES),
        name="b_layer",
    )(sinks, x1, past_k, past_v, kv_g.reshape(1, d), b_g.reshape(1, d), w_kv,
      jnp.tile(k_norm, N_KV_HEADS).reshape(1, kvw), _pack_rows(_head_mean_matrix(kvw)), w_in,
      jnp.tile(q_norm, N_HEADS).reshape(1, d), _pack_rows(_head_mean_matrix(seg)), _alibi_bias(chunk),
      w_out)


def kernel(x_prompt, x_sample, cache_k, cache_v, state_conv, state_rglru, a_norm, a_w_in, a_conv_w,
           a_conv_b, a_gate_a_w, a_gate_a_b, a_gate_x_w, a_gate_x_b, a_lambda, a_w_out, kv_norm, w_kv,
           k_norm, b_norm, b_w_in, q_norm, sinks, b_w_out):
    assert a_norm.shape[0] == 1 and b_norm.shape[0] == 1
    d = x_prompt.shape[-1]
    kvw = N_KV_HEADS * HEAD_DIM
    taps = CONV_WIDTH - 1
    a_w = (a_norm[0], _pack_rows(a_w_in[0]), a_conv_w[0], a_conv_b[0], _pack_rows(a_gate_a_w[0]),
           a_gate_a_b[0], _pack_rows(a_gate_x_w[0]), a_gate_x_b[0], a_lambda[0], _pack_rows(a_w_out[0]))
    b_w = (kv_norm, b_norm[0], _pack_rows(w_kv), k_norm, _pack_rows(b_w_in[0]), q_norm[0], sinks[0],
           _pack_rows(b_w_out[0]))

    def b_group(x1, past_k, past_v, has_past, tile_t, chunk):
        bsz = x1.shape[0]
        y, k_buf, v_buf = _b_layer(x1, past_k.reshape(bsz, WINDOW, kvw), past_v.reshape(bsz, WINDOW, kvw),
                                   *b_w, tile_t=tile_t, chunk=chunk, has_past=has_past)
        shape_kv = (bsz, WINDOW, N_KV_HEADS, HEAD_DIM)
        return y, k_buf.reshape(shape_kv), v_buf.reshape(shape_kv)

    bp = x_prompt.shape[0]
    x1_p, tail_p, hl_p = _a_layer(x_prompt, jnp.zeros((bp, taps * SUBLANES, d), F32),
                                  jnp.zeros((bp, SUBLANES, d), F32), *a_w, rows=256, chained=True,
                                  seqs_per_step=2)
    p_conv = tail_p[:, SUBLANES - 1::SUBLANES, :][None]
    p_h = hl_p[:, 0, :][None]
    zeros_kv = jnp.zeros((bp, WINDOW, kvw), F32)
    y_p, p_k, p_v = b_group(x1_p, zeros_kv, zeros_kv, False, 256, CHUNK)

    bs, ts, _ = x_sample.shape
    assert bs == SUBLANES
    hist_s = jnp.transpose(state_conv[0], (1, 0, 2)).reshape(1, taps * bs, d)
    x1_s, tail_s, hl_s = _a_layer(x_sample.reshape(1, bs * ts, d), hist_s, state_rglru, *a_w,
                                  rows=bs * ts, chained=False, seqs_per_step=1)
    s_conv = jnp.transpose(tail_s.reshape(taps, bs, d), (1, 0, 2))[None]
    s_h = hl_s
    y_s, s_k, s_v = b_group(x1_s.reshape(bs, ts, d), cache_k, cache_v, True, ts, ts)
    return (y_p, y_s, p_k, p_v, p_conv, p_h, s_k, s_v, s_conv, s_h)
```

```python
import functools

import jax
import jax.numpy as jnp
from jax import lax
from jax.experimental import pallas as pl
from jax.experimental.pallas import tpu as pltpu

F32 = jnp.float32
BF16 = jnp.bfloat16

EPS = 1e-6
CONV_WIDTH = 4
N_RG_BLOCKS = 4
RG_C = 8.0
N_HEADS = 16
HEAD_DIM = 64
N_KV_HEADS = 2
GROUP = N_HEADS // N_KV_HEADS
WINDOW = 128
CHUNK = 64

SUBLANES = 8
LANES = 128
VMEM_LIMIT_BYTES = 56 * 1024 * 1024

PAIRS = GROUP // 2
KEY_SLOTS = 2 * LANES
NEG = -1e30
A_SEQ_LAG = 2600


def _sigmoid(x):
    return jax.nn.sigmoid(x)


def _bf16(packed):
    return pltpu.bitcast(packed, BF16)


def _u32(x_bf16):
    return pltpu.bitcast(x_bf16, jnp.uint32)


def _segment_permutation(rows):
    steps = rows // SUBLANES
    r = jnp.arange(rows)
    src = (r % SUBLANES) * steps + r // SUBLANES
    return (src[:, None] == jnp.arange(rows)[None, :]).astype(BF16)


def _interleave(gens, lag):
    clock = {k: k * lag for k in range(len(gens))}
    while clock:
        k = min(clock, key=lambda j: (clock[j], j))
        try:
            clock[k] += next(gens[k])
        except StopIteration:
            del clock[k]


def _a_layer_kernel(x_ref, hist_ref, h0_ref, pm_ref, pmt_ref, g_ref, win_ref, cw_ref, cb_ref,
                    wa_ref, ba_ref, wx_ref, bx_ref, lam_ref, wout_ref,
                    x1_ref, tail_ref, hlast_ref,
                    xbe_ref, prev_ref, hc_ref, *, rows, d, chained):
    gens = [_a_layer_tile(x_ref.at[n], hist_ref.at[n], h0_ref.at[n], pm_ref, pmt_ref, g_ref, win_ref,
                          cw_ref, cb_ref, wa_ref, ba_ref, wx_ref, bx_ref, lam_ref, wout_ref,
                          x1_ref.at[n], tail_ref.at[n], hlast_ref.at[n],
                          xbe_ref.at[n], prev_ref.at[n], hc_ref.at[n], rows=rows, d=d, chained=chained)
            for n in range(x_ref.shape[0])]
    _interleave(gens, lag=A_SEQ_LAG)


def _a_layer_tile(x_ref, hist_ref, h0_ref, pm_ref, pmt_ref, g_ref, win_ref, cw_ref, cb_ref,
                  wa_ref, ba_ref, wx_ref, bx_ref, lam_ref, wout_ref,
                  x1_ref, tail_ref, hlast_ref,
                  xbe_ref, prev_ref, hc_ref, *, rows, d, chained):
    t = pl.program_id(1)
    steps = rows // SUBLANES
    taps = CONV_WIDTH - 1
    hdr = taps * SUBLANES
    blk = d // N_RG_BLOCKS
    sub = lax.broadcasted_iota(jnp.int32, (SUBLANES, d), 0)

    if chained:
        @pl.when(t == 0)
        def _():
            prev_ref[...] = hist_ref[...]
            hc_ref[...] = h0_ref[...]

    x = x_ref[...]
    ms = jnp.mean(x * x, axis=-1, keepdims=True)
    xn = (x * lax.rsqrt(ms + EPS) * g_ref[...]).astype(BF16)
    xn = jnp.dot(pm_ref[...], xn, preferred_element_type=F32).astype(BF16)
    yield 600

    for n in range(N_RG_BLOCKS):
        xbe_ref[hdr:hdr + rows, n * blk:(n + 1) * blk] = jnp.dot(
            xn, win_ref[:, n * blk:(n + 1) * blk], preferred_element_type=F32)
        yield 256
    gate = []
    for n in range(N_RG_BLOCKS):
        gate.append(jnp.dot(xn, win_ref[:, d + n * blk:d + (n + 1) * blk],
                            preferred_element_type=F32))
        yield 256

    if chained:
        for i in range(taps):
            r0 = hdr + (steps - taps + i) * SUBLANES
            before = pltpu.roll(prev_ref[i * SUBLANES:(i + 1) * SUBLANES, :], 1, 0)
            within = pltpu.roll(xbe_ref[r0:r0 + SUBLANES, :], 1, 0)
            xbe_ref[i * SUBLANES:(i + 1) * SUBLANES, :] = jnp.where(sub == 0, before, within)
        prev_ref[...] = xbe_ref[rows:rows + hdr, :]
    else:
        xbe_ref[0:hdr, :] = hist_ref[...]
    tail_ref[...] = xbe_ref[rows:rows + hdr, :]

    log_decay = RG_C * jax.nn.log_sigmoid(lam_ref[...])
    a_blk, b_blk = [], []
    for n in range(N_RG_BLOCKS):
        cs = slice(n * blk, (n + 1) * blk)
        xc = cb_ref[:, cs]
        for k in range(CONV_WIDTH):
            off = k * SUBLANES
            xc = xc + xbe_ref[off:off + rows, cs] * cw_ref[k:k + 1, cs]
        xcb = xc.astype(BF16)
        r = _sigmoid(jnp.dot(xcb, wa_ref[n], preferred_element_type=F32) + ba_ref[:, cs])
        i = _sigmoid(jnp.dot(xcb, wx_ref[n], preferred_element_type=F32) + bx_ref[:, cs])
        log_a = r * log_decay[:, cs]
        a = jnp.exp(log_a)
        one_minus_a2 = -jnp.tanh(log_a) * (a * a + 1.0)
        root = jnp.where(one_minus_a2 > 0.0, one_minus_a2 * lax.rsqrt(one_minus_a2), 0.0)
        a_blk.append(a)
        b_blk.append(root * (i * xc))
        yield 550
    a = jnp.concatenate(a_blk, axis=1)
    b = jnp.concatenate(b_blk, axis=1)

    h = b[0:SUBLANES, :]
    c = a[0:SUBLANES, :]
    h_loc, a_cum = [h], [c]
    for p in range(1, steps):
        a_p = a[p * SUBLANES:(p + 1) * SUBLANES, :]
        h = a_p * h + b[p * SUBLANES:(p + 1) * SUBLANES, :]
        c = c * a_p
        h_loc.append(h)
        a_cum.append(c)

    if chained:
        a_seg, b_seg = c, h
        for dist in (1, 2, 4):
            a_prev = pltpu.roll(a_seg, dist, 0)
            b_prev = pltpu.roll(b_seg, dist, 0)
            m = sub >= dist
            b_seg = jnp.where(m, a_seg * b_prev + b_seg, b_seg)
            a_seg = jnp.where(m, a_seg * a_prev, a_seg)
        after = a_seg * hc_ref[...] + b_seg
        h_in = jnp.where(sub == 0, hc_ref[...], pltpu.roll(after, 1, 0))
        h_end = jnp.broadcast_to(after[SUBLANES - 1:SUBLANES, :], (SUBLANES, d))
        hc_ref[...] = h_end
    else:
        h_in = h0_ref[...]
    h_all = [h_loc[p] + a_cum[p] * h_in for p in range(steps)]
    hlast_ref[...] = h_end if chained else h_all[steps - 1]
    yield 450

    gate = jnp.concatenate(gate, axis=1)
    hg = (jnp.concatenate(h_all, axis=0) * (gate * _sigmoid(gate))).astype(BF16)
    hg = jnp.dot(pmt_ref[...], hg, preferred_element_type=F32).astype(BF16)
    yield 450
    for n in range(N_RG_BLOCKS):
        cs = slice(n * blk, (n + 1) * blk)
        x1_ref[:, cs] = x[:, cs] + jnp.dot(hg, wout_ref[:, cs], preferred_element_type=F32)
        yield 256


def _a_layer(x, hist, h0, g, w_in, conv_w, conv_b, wa, ba, wx, bx, lam, w_out, *, rows, chained,
             seqs_per_step):
    n_seq, n_rows, d = x.shape
    taps = CONV_WIDTH - 1
    assert n_rows % rows == 0 and rows % SUBLANES == 0 and rows // SUBLANES >= taps
    assert n_seq % seqs_per_step == 0
    n_t = n_rows // rows
    ns = seqs_per_step
    assert chained or n_t == 1
    kern = functools.partial(_a_layer_kernel, rows=rows, d=d, chained=chained)
    row = lambda a: a.reshape(1, d)
    full = lambda shape: pl.BlockSpec(shape, lambda b, t: (0,) * len(shape))
    perm = _segment_permutation(rows)
    nb = N_RG_BLOCKS
    return pl.pallas_call(
        kern,
        grid=(n_seq // ns, n_t),
        in_specs=[
            pl.BlockSpec((ns, rows, d), lambda b, t: (b, t, 0)),
            pl.BlockSpec((ns, taps * SUBLANES, d), lambda b, t: (b, 0, 0)),
            pl.BlockSpec((ns, SUBLANES, d), lambda b, t: (b, 0, 0)),
            full((rows, rows)),
            full((rows, rows)),
            full((1, d)),
            full((d, 2 * d)),
            full((CONV_WIDTH, d)),
            full((1, d)),
            full((nb, d // nb, d // nb)),
            full((1, d)),
            full((nb, d // nb, d // nb)),
            full((1, d)),
            full((1, d)),
            full((d, d)),
        ],
        out_specs=[
            pl.BlockSpec((ns, rows, d), lambda b, t: (b, t, 0)),
            pl.BlockSpec((ns, taps * SUBLANES, d), lambda b, t: (b, 0, 0)),
            pl.BlockSpec((ns, SUBLANES, d), lambda b, t: (b, 0, 0)),
        ],
        out_shape=[
            jax.ShapeDtypeStruct((n_seq, n_rows, d), F32),
            jax.ShapeDtypeStruct((n_seq, taps * SUBLANES, d), F32),
            jax.ShapeDtypeStruct((n_seq, SUBLANES, d), F32),
        ],
        scratch_shapes=[
            pltpu.VMEM((ns, taps * SUBLANES + rows, d), F32),
            pltpu.VMEM((ns, taps * SUBLANES, d), F32),
            pltpu.VMEM((ns, SUBLANES, d), F32),
        ],
        compiler_params=pltpu.CompilerParams(
            dimension_semantics=("arbitrary", "arbitrary"),
            vmem_limit_bytes=VMEM_LIMIT_BYTES),
        name="a_layer",
    )(x, hist, h0, perm, perm.T, row(g), w_in, conv_w, row(conv_b), wa, row(ba), wx, row(bx),
      row(lam), w_out)


def _b_layer_kernel(sink_ref, x1_ref, pk_ref, pv_ref, kvg_ref, bg_ref, wkv_ref, kn_ref, e128_ref,
                    win_ref, qn_ref, e256_ref, bias_ref, wout_ref,
                    y_ref, ko_ref, vo_ref,
                    kf_ref, vf_ref, kx_ref, vx_ref, o_ref,
                    *, tile_t, chunk, d, n_t, has_past):
    t = pl.program_id(1)
    n_c = tile_t // chunk
    rows_x = 2 * kx_ref.shape[1]
    lane = lax.broadcasted_iota(jnp.int32, (1, LANES), 1)
    lo = lane < HEAD_DIM

    def expand_k(kfull):
        rolled = pltpu.roll(kfull, HEAD_DIM, 1)
        zero = jnp.zeros_like(kfull)
        return (jnp.where(lo, kfull, zero), jnp.where(lo, zero, rolled),
                jnp.where(lo, rolled, zero), jnp.where(lo, zero, kfull))

    def store_expanded(r0, nrows, kfull, vfull):
        for j, (kv, vv) in enumerate(zip(expand_k(kfull), expand_k(vfull))):
            kx_ref[j, r0 // 2:(r0 + nrows) // 2, :] = _u32(kv.astype(BF16))
            vx_ref[j, r0 // 2:(r0 + nrows) // 2, 0:LANES] = _u32(vv.astype(BF16))

    @pl.when(t == 0)
    def _():
        ones_lo = jnp.broadcast_to(jnp.where(lo, 1.0, 0.0), (rows_x, LANES)).astype(BF16)
        ones_hi = jnp.broadcast_to(jnp.where(lo, 0.0, 1.0), (rows_x, LANES)).astype(BF16)
        for j in range(4):
            vx_ref[j, :, LANES:2 * LANES] = _u32(ones_lo if j % 2 == 0 else ones_hi)
            kx_ref[j, WINDOW // 2:rows_x // 2, :] = jnp.zeros(((rows_x - WINDOW) // 2, LANES), jnp.uint32)
            vx_ref[j, WINDOW // 2:rows_x // 2, 0:LANES] = jnp.zeros(((rows_x - WINDOW) // 2, LANES), jnp.uint32)
        kf_ref[0:WINDOW, :] = pk_ref[0]
        vf_ref[0:WINDOW, :] = pv_ref[0]
        store_expanded(0, WINDOW, pk_ref[0], pv_ref[0])

    x1 = x1_ref[0]
    ms = jnp.mean(x1 * x1, axis=-1, keepdims=True)
    xhat = x1 * lax.rsqrt(ms + EPS)
    xk = (xhat * kvg_ref[...]).astype(BF16)
    xq = (xhat * bg_ref[...]).astype(BF16)

    kv = jnp.dot(xk, wkv_ref[...], preferred_element_type=F32)
    k_raw = kv[:, 0:LANES]
    v_new = kv[:, LANES:2 * LANES]
    k_ms = jnp.dot((k_raw * k_raw).astype(BF16), e128_ref[...], preferred_element_type=F32)
    k_new = k_raw * lax.rsqrt(k_ms + EPS) * kn_ref[...]
    kf_ref[WINDOW:WINDOW + tile_t, :] = k_new
    vf_ref[WINDOW:WINDOW + tile_t, :] = v_new
    store_expanded(WINDOW, tile_t, k_new, v_new)

    u = jnp.dot(xq, win_ref[...], preferred_element_type=F32)
    q_raw = u[:, :d]
    gate = u[:, d:]
    q_sq = (q_raw * q_raw).astype(BF16)
    seg = e256_ref.shape[0]
    q_ms = jnp.concatenate(
        [jnp.dot(q_sq[:, j * seg:(j + 1) * seg], e256_ref[...], preferred_element_type=F32)
         for j in range(d // seg)], axis=1)
    q = (q_raw * lax.rsqrt(q_ms + EPS) * qn_ref[...] * (HEAD_DIM ** -0.5)).astype(BF16)

    col = lax.broadcasted_iota(jnp.int32, (1, 2 * KEY_SLOTS), 1) % KEY_SLOTS
    for c in range(n_c):
        r0 = c * chunk
        for g in range(N_KV_HEADS):
            band = slice(r0 // 2, (r0 + KEY_SLOTS) // 2)
            k_bd = _bf16(jnp.concatenate([kx_ref[2 * g, band, :], kx_ref[2 * g + 1, band, :]], axis=0))
            v_bd = _bf16(jnp.concatenate([vx_ref[2 * g, band, :], vx_ref[2 * g + 1, band, :]], axis=0))
            q_g = jnp.concatenate(
                [q[r0:r0 + chunk, (g * PAIRS + p) * LANES:(g * PAIRS + p + 1) * LANES]
                 for p in range(PAIRS)], axis=0)
            s = lax.dot_general(q_g, k_bd, (((1,), (1,)), ((), ())),
                                preferred_element_type=F32)
            s = s + bias_ref[g]
            if not has_past and r0 < WINDOW:
                n_dead = jnp.where(t == 0, WINDOW - r0, 0)
                s = s + jnp.where(col < n_dead, NEG, 0.0)
            p_blocks, sink_terms = [], []
            for p in range(PAIRS):
                row_p, st_p = [], []
                for e in range(2):
                    sink = sink_ref[g * GROUP + 2 * p + e]
                    sb = s[p * chunk:(p + 1) * chunk, e * KEY_SLOTS:(e + 1) * KEY_SLOTS]
                    m = jnp.maximum(jnp.max(sb, axis=-1, keepdims=True), sink)
                    row_p.append(jnp.exp(sb - m).astype(BF16))
                    st_p.append(jnp.exp(sink - m))
                p_blocks.append(jnp.concatenate(row_p, axis=1))
                sink_terms.append(jnp.where(lo, st_p[0], st_p[1]))
            prob = jnp.concatenate(p_blocks, axis=0)
            ov = jnp.dot(prob, v_bd, preferred_element_type=F32)
            den = ov[:, LANES:2 * LANES] + jnp.concatenate(sink_terms, axis=0)
            o = ov[:, 0:LANES] / den
            for p in range(PAIRS):
                c0 = (g * PAIRS + p) * LANES
                o_ref[r0:r0 + chunk, c0:c0 + LANES] = o[p * chunk:(p + 1) * chunk, :]

    og = (o_ref[...] * (gate * _sigmoid(gate))).astype(BF16)
    y_ref[0] = x1 + jnp.dot(og, wout_ref[...], preferred_element_type=F32)

    @pl.when(t == n_t - 1)
    def _():
        ko_ref[0] = kf_ref[tile_t:tile_t + WINDOW, :]
        vo_ref[0] = vf_ref[tile_t:tile_t + WINDOW, :]

    if n_t > 1:
        kf_ref[0:WINDOW, :] = kf_ref[tile_t:tile_t + WINDOW, :]
        vf_ref[0:WINDOW, :] = vf_ref[tile_t:tile_t + WINDOW, :]
        for j in range(4):
            kx_ref[j, 0:WINDOW // 2, :] = kx_ref[j, tile_t // 2:(tile_t + WINDOW) // 2, :]
            vx_ref[j, 0:WINDOW // 2, 0:LANES] = vx_ref[j, tile_t // 2:(tile_t + WINDOW) // 2, 0:LANES]


def _alibi_bias(chunk):
    slopes = 2.0 ** (-8.0 * jnp.arange(1, N_HEADS + 1, dtype=F32) / N_HEADS)
    slopes = slopes.reshape(N_KV_HEADS, PAIRS, 1, 2, 1)
    qi = jnp.arange(chunk, dtype=F32).reshape(1, 1, chunk, 1, 1)
    sj = jnp.arange(KEY_SLOTS, dtype=F32).reshape(1, 1, 1, 1, KEY_SLOTS)
    dist = jnp.abs(qi + WINDOW - sj)
    bias = jnp.where(sj < WINDOW + chunk, -(slopes * dist), NEG)
    return bias.reshape(N_KV_HEADS, PAIRS * chunk, 2 * KEY_SLOTS)


def _head_mean_matrix(n):
    idx = jnp.arange(n) // HEAD_DIM
    return jnp.where(idx[:, None] == idx[None, :], 1.0 / HEAD_DIM, 0.0).astype(BF16)


def _b_layer(x1, past_k, past_v, kv_g, b_g, w_kv, k_norm, w_in, q_norm, sinks, w_out,
             *, tile_t, chunk, has_past):
    bsz, seq, d = x1.shape
    kvw = N_KV_HEADS * HEAD_DIM
    assert kvw == LANES and d == N_HEADS * HEAD_DIM
    assert seq % tile_t == 0 and tile_t % chunk == 0 and WINDOW + chunk <= KEY_SLOTS
    n_t = seq // tile_t
    assert n_t == 1 or tile_t >= WINDOW
    rows_x = tile_t - chunk + KEY_SLOTS
    kern = functools.partial(_b_layer_kernel, tile_t=tile_t, chunk=chunk, d=d, n_t=n_t,
                             has_past=has_past)
    full = lambda shape: pl.BlockSpec(shape, lambda b, t: (0,) * len(shape))
    seg = 2 * LANES
    return pl.pallas_call(
        kern,
        grid=(bsz, n_t),
        in_specs=[
            pl.BlockSpec(memory_space=pltpu.SMEM),
            pl.BlockSpec((1, tile_t, d), lambda b, t: (b, t, 0)),
            pl.BlockSpec((1, WINDOW, kvw), lambda b, t: (b, 0, 0)),
            pl.BlockSpec((1, WINDOW, kvw), lambda b, t: (b, 0, 0)),
            full((1, d)),
            full((1, d)),
            full((d, 2 * kvw)),
            full((1, kvw)),
            full((kvw, kvw)),
            full((d, 2 * d)),
            full((1, d)),
            full((seg, seg)),
            full((N_KV_HEADS, PAIRS * chunk, 2 * KEY_SLOTS)),
            full((d, d)),
        ],
        out_specs=[
            pl.BlockSpec((1, tile_t, d), lambda b, t: (b, t, 0)),
            pl.BlockSpec((1, WINDOW, kvw), lambda b, t: (b, 0, 0)),
            pl.BlockSpec((1, WINDOW, kvw), lambda b, t: (b, 0, 0)),
        ],
        out_shape=[
            jax.ShapeDtypeStruct((bsz, seq, d), F32),
            jax.ShapeDtypeStruct((bsz, WINDOW, kvw), F32),
            jax.ShapeDtypeStruct((bsz, WINDOW, kvw), F32),
        ],
        scratch_shapes=[
            pltpu.VMEM((WINDOW + tile_t, kvw), F32),
            pltpu.VMEM((WINDOW + tile_t, kvw), F32),
            pltpu.VMEM((4, rows_x // 2, LANES), jnp.uint32),
            pltpu.VMEM((4, rows_x // 2, 2 * LANES), jnp.uint32),
            pltpu.VMEM((tile_t, d), F32),
        ],
        compiler_params=pltpu.CompilerParams(
            dimension_semantics=("arbitrary", "arbitrary"),
            vmem_limit_bytes=VMEM_LIMIT_BYTES),
        name="b_layer",
    )(sinks, x1, past_k, past_v, kv_g.reshape(1, d), b_g.reshape(1, d), w_kv,
      jnp.tile(k_norm, N_KV_HEADS).reshape(1, kvw), _head_mean_matrix(kvw), w_in,
      jnp.tile(q_norm, N_HEADS).reshape(1, d), _head_mean_matrix(seg), _alibi_bias(chunk), w_out)


def kernel(x_prompt, x_sample, cache_k, cache_v, state_conv, state_rglru, a_norm, a_w_in, a_conv_w,
           a_conv_b, a_gate_a_w, a_gate_a_b, a_gate_x_w, a_gate_x_b, a_lambda, a_w_out, kv_norm, w_kv,
           k_norm, b_norm, b_w_in, q_norm, sinks, b_w_out):
    assert a_norm.shape[0] == 1 and b_norm.shape[0] == 1
    d = x_prompt.shape[-1]
    kvw = N_KV_HEADS * HEAD_DIM
    taps = CONV_WIDTH - 1
    a_w = (a_norm[0], a_w_in[0].astype(BF16), a_conv_w[0], a_conv_b[0], a_gate_a_w[0].astype(BF16),
           a_gate_a_b[0], a_gate_x_w[0].astype(BF16), a_gate_x_b[0], a_lambda[0],
           a_w_out[0].astype(BF16))
    b_w = (kv_norm, b_norm[0], w_kv.astype(BF16), k_norm, b_w_in[0].astype(BF16), q_norm[0], sinks[0],
           b_w_out[0].astype(BF16))

    def b_group(x1, past_k, past_v, has_past, tile_t, chunk):
        bsz = x1.shape[0]
        y, k_buf, v_buf = _b_layer(x1, past_k.reshape(bsz, WINDOW, kvw), past_v.reshape(bsz, WINDOW, kvw),
                                   *b_w, tile_t=tile_t, chunk=chunk, has_past=has_past)
        shape_kv = (bsz, WINDOW, N_KV_HEADS, HEAD_DIM)
        return y, k_buf.reshape(shape_kv), v_buf.reshape(shape_kv)

    bp = x_prompt.shape[0]
    x1_p, tail_p, hl_p = _a_layer(x_prompt, jnp.zeros((bp, taps * SUBLANES, d), F32),
                                  jnp.zeros((bp, SUBLANES, d), F32), *a_w, rows=256, chained=True,
                                  seqs_per_step=2)
    p_conv = tail_p[:, SUBLANES - 1::SUBLANES, :][None]
    p_h = hl_p[:, 0, :][None]
    zeros_kv = jnp.zeros((bp, WINDOW, kvw), F32)
    y_p, p_k, p_v = b_group(x1_p, zeros_kv, zeros_kv, False, 512, CHUNK)

    bs, ts, _ = x_sample.shape
    assert bs == SUBLANES
    hist_s = jnp.transpose(state_conv[0], (1, 0, 2)).reshape(1, taps * bs, d)
    x1_s, tail_s, hl_s = _a_layer(x_sample.reshape(1, bs * ts, d), hist_s, state_rglru, *a_w,
                                  rows=bs * ts, chained=False, seqs_per_step=1)
    s_conv = jnp.transpose(tail_s.reshape(taps, bs, d), (1, 0, 2))[None]
    s_h = hl_s
    y_s, s_k, s_v = b_group(x1_s.reshape(bs, ts, d), cache_k, cache_v, True, ts, ts)
    return (y_p, y_s, p_k, p_v, p_conv, p_h, s_k, s_v, s_conv, s_h)
```

```python
import functools

import jax
import jax.numpy as jnp
from jax import lax
from jax.experimental import pallas as pl
from jax.experimental.pallas import tpu as pltpu

F32 = jnp.float32
BF16 = jnp.bfloat16

EPS = 1e-6
CONV_WIDTH = 4
N_RG_BLOCKS = 4
RG_C = 8.0
N_HEADS = 16
HEAD_DIM = 64
N_KV_HEADS = 2
GROUP = N_HEADS // N_KV_HEADS
WINDOW = 128
CHUNK = 64

SUBLANES = 8
LANES = 128
VMEM_LIMIT_BYTES = 56 * 1024 * 1024

PAIRS = GROUP // 2
KEY_SLOTS = 2 * LANES
NEG = -1e30
LOG2E = 1.4426950408889634
A_SEQ_LAG = 2600


def _sigmoid(x):
    return jax.nn.sigmoid(x)


def _bf16(packed):
    return pltpu.bitcast(packed, BF16)


def _u32(x_bf16):
    return pltpu.bitcast(x_bf16, jnp.uint32)


def _segment_permutation(rows):
    steps = rows // SUBLANES
    r = jnp.arange(rows)
    src = (r % SUBLANES) * steps + r // SUBLANES
    return (src[:, None] == jnp.arange(rows)[None, :]).astype(BF16)


def _interleave(gens, lag):
    clock = {k: k * lag for k in range(len(gens))}
    while clock:
        k = min(clock, key=lambda j: (clock[j], j))
        try:
            clock[k] += next(gens[k])
        except StopIteration:
            del clock[k]


def _a_layer_kernel(x_ref, hist_ref, h0_ref, pm_ref, pmt_ref, g_ref, win_ref, cw_ref, cb_ref,
                    wa_ref, ba_ref, wx_ref, bx_ref, lam_ref, wout_ref,
                    x1_ref, tail_ref, hlast_ref,
                    xbe_ref, prev_ref, hc_ref, *, rows, d, chained):
    gens = [_a_layer_tile(x_ref.at[n], hist_ref.at[n], h0_ref.at[n], pm_ref, pmt_ref, g_ref, win_ref,
                          cw_ref, cb_ref, wa_ref, ba_ref, wx_ref, bx_ref, lam_ref, wout_ref,
                          x1_ref.at[n], tail_ref.at[n], hlast_ref.at[n],
                          xbe_ref.at[n], prev_ref.at[n], hc_ref.at[n], rows=rows, d=d, chained=chained)
            for n in range(x_ref.shape[0])]
    _interleave(gens, lag=A_SEQ_LAG)


def _a_layer_tile(x_ref, hist_ref, h0_ref, pm_ref, pmt_ref, g_ref, win_ref, cw_ref, cb_ref,
                  wa_ref, ba_ref, wx_ref, bx_ref, lam_ref, wout_ref,
                  x1_ref, tail_ref, hlast_ref,
                  xbe_ref, prev_ref, hc_ref, *, rows, d, chained):
    t = pl.program_id(1)
    steps = rows // SUBLANES
    taps = CONV_WIDTH - 1
    hdr = taps * SUBLANES
    blk = d // N_RG_BLOCKS
    sub = lax.broadcasted_iota(jnp.int32, (SUBLANES, d), 0)

    if chained:
        @pl.when(t == 0)
        def _():
            prev_ref[...] = hist_ref[...]
            hc_ref[...] = h0_ref[...]

    x = x_ref[...]
    ms = jnp.mean(x * x, axis=-1, keepdims=True)
    xn = (x * lax.rsqrt(ms + EPS) * g_ref[...]).astype(BF16)
    xn = jnp.dot(pm_ref[...], xn, preferred_element_type=F32).astype(BF16)
    yield 600

    for n in range(N_RG_BLOCKS):
        xbe_ref[hdr:hdr + rows, n * blk:(n + 1) * blk] = jnp.dot(
            xn, win_ref[:, n * blk:(n + 1) * blk], preferred_element_type=F32)
        yield 256
    gate = []
    for n in range(N_RG_BLOCKS):
        gate.append(jnp.dot(xn, win_ref[:, d + n * blk:d + (n + 1) * blk],
                            preferred_element_type=F32))
        yield 256

    if chained:
        for i in range(taps):
            r0 = hdr + (steps - taps + i) * SUBLANES
            before = pltpu.roll(prev_ref[i * SUBLANES:(i + 1) * SUBLANES, :], 1, 0)
            within = pltpu.roll(xbe_ref[r0:r0 + SUBLANES, :], 1, 0)
            xbe_ref[i * SUBLANES:(i + 1) * SUBLANES, :] = jnp.where(sub == 0, before, within)
        prev_ref[...] = xbe_ref[rows:rows + hdr, :]
    else:
        xbe_ref[0:hdr, :] = hist_ref[...]
    tail_ref[...] = xbe_ref[rows:rows + hdr, :]

    log_decay = RG_C * jax.nn.log_sigmoid(lam_ref[...])
    a_blk, b_blk = [], []
    for n in range(N_RG_BLOCKS):
        cs = slice(n * blk, (n + 1) * blk)
        xc = cb_ref[:, cs]
        for k in range(CONV_WIDTH):
            off = k * SUBLANES
            xc = xc + xbe_ref[off:off + rows, cs] * cw_ref[k:k + 1, cs]
        xcb = xc.astype(BF16)
        r = _sigmoid(jnp.dot(xcb, wa_ref[n], preferred_element_type=F32) + ba_ref[:, cs])
        i = _sigmoid(jnp.dot(xcb, wx_ref[n], preferred_element_type=F32) + bx_ref[:, cs])
        log_a = r * log_decay[:, cs]
        a = jnp.exp(log_a)
        one_minus_a2 = -jnp.tanh(log_a) * (a * a + 1.0)
        root = jnp.where(one_minus_a2 > 0.0, one_minus_a2 * lax.rsqrt(one_minus_a2), 0.0)
        a_blk.append(a)
        b_blk.append(root * (i * xc))
        yield 550
    a = jnp.concatenate(a_blk, axis=1)
    b = jnp.concatenate(b_blk, axis=1)

    h = b[0:SUBLANES, :]
    c = a[0:SUBLANES, :]
    h_loc, a_cum = [h], [c]
    for p in range(1, steps):
        a_p = a[p * SUBLANES:(p + 1) * SUBLANES, :]
        h = a_p * h + b[p * SUBLANES:(p + 1) * SUBLANES, :]
        c = c * a_p
        h_loc.append(h)
        a_cum.append(c)

    if chained:
        a_seg, b_seg = c, h
        for dist in (1, 2, 4):
            a_prev = pltpu.roll(a_seg, dist, 0)
            b_prev = pltpu.roll(b_seg, dist, 0)
            m = sub >= dist
            b_seg = jnp.where(m, a_seg * b_prev + b_seg, b_seg)
            a_seg = jnp.where(m, a_seg * a_prev, a_seg)
        after = a_seg * hc_ref[...] + b_seg
        h_in = jnp.where(sub == 0, hc_ref[...], pltpu.roll(after, 1, 0))
        h_end = jnp.broadcast_to(after[SUBLANES - 1:SUBLANES, :], (SUBLANES, d))
        hc_ref[...] = h_end
    else:
        h_in = h0_ref[...]
    h_all = [h_loc[p] + a_cum[p] * h_in for p in range(steps)]
    hlast_ref[...] = h_end if chained else h_all[steps - 1]
    yield 450

    gate = jnp.concatenate(gate, axis=1)
    hg = (jnp.concatenate(h_all, axis=0) * (gate * _sigmoid(gate))).astype(BF16)
    hg = jnp.dot(pmt_ref[...], hg, preferred_element_type=F32).astype(BF16)
    yield 450
    for n in range(N_RG_BLOCKS):
        cs = slice(n * blk, (n + 1) * blk)
        x1_ref[:, cs] = x[:, cs] + jnp.dot(hg, wout_ref[:, cs], preferred_element_type=F32)
        yield 256


def _a_layer(x, hist, h0, g, w_in, conv_w, conv_b, wa, ba, wx, bx, lam, w_out, *, rows, chained,
             seqs_per_step):
    n_seq, n_rows, d = x.shape
    taps = CONV_WIDTH - 1
    assert n_rows % rows == 0 and rows % SUBLANES == 0 and rows // SUBLANES >= taps
    assert n_seq % seqs_per_step == 0
    n_t = n_rows // rows
    ns = seqs_per_step
    assert chained or n_t == 1
    kern = functools.partial(_a_layer_kernel, rows=rows, d=d, chained=chained)
    row = lambda a: a.reshape(1, d)
    full = lambda shape: pl.BlockSpec(shape, lambda b, t: (0,) * len(shape))
    perm = _segment_permutation(rows)
    nb = N_RG_BLOCKS
    return pl.pallas_call(
        kern,
        grid=(n_seq // ns, n_t),
        in_specs=[
            pl.BlockSpec((ns, rows, d), lambda b, t: (b, t, 0)),
            pl.BlockSpec((ns, taps * SUBLANES, d), lambda b, t: (b, 0, 0)),
            pl.BlockSpec((ns, SUBLANES, d), lambda b, t: (b, 0, 0)),
            full((rows, rows)),
            full((rows, rows)),
            full((1, d)),
            full((d, 2 * d)),
            full((CONV_WIDTH, d)),
            full((1, d)),
            full((nb, d // nb, d // nb)),
            full((1, d)),
            full((nb, d // nb, d // nb)),
            full((1, d)),
            full((1, d)),
            full((d, d)),
        ],
        out_specs=[
            pl.BlockSpec((ns, rows, d), lambda b, t: (b, t, 0)),
            pl.BlockSpec((ns, taps * SUBLANES, d), lambda b, t: (b, 0, 0)),
            pl.BlockSpec((ns, SUBLANES, d), lambda b, t: (b, 0, 0)),
        ],
        out_shape=[
            jax.ShapeDtypeStruct((n_seq, n_rows, d), F32),
            jax.ShapeDtypeStruct((n_seq, taps * SUBLANES, d), F32),
            jax.ShapeDtypeStruct((n_seq, SUBLANES, d), F32),
        ],
        scratch_shapes=[
            pltpu.VMEM((ns, taps * SUBLANES + rows, d), F32),
            pltpu.VMEM((ns, taps * SUBLANES, d), F32),
            pltpu.VMEM((ns, SUBLANES, d), F32),
        ],
        compiler_params=pltpu.CompilerParams(
            dimension_semantics=("arbitrary", "arbitrary"),
            vmem_limit_bytes=VMEM_LIMIT_BYTES),
        name="a_layer",
    )(x, hist, h0, perm, perm.T, row(g), w_in, conv_w, row(conv_b), wa, row(ba), wx, row(bx),
      row(lam), w_out)


def _b_layer_kernel(x1_ref, pk_ref, pv_ref, kvg_ref, bg_ref, wkv_ref, kn_ref, e128_ref,
                    win_ref, qn_ref, e256_ref, bias_ref, wout_ref,
                    y_ref, ko_ref, vo_ref,
                    kf_ref, vf_ref, kx_ref, vx_ref, o_ref,
                    *, tile_t, chunk, d, n_t, has_past):
    t = pl.program_id(1)
    n_c = tile_t // chunk
    rows_x = 2 * kx_ref.shape[1]
    n_keys = WINDOW + chunk
    data = WINDOW + tile_t
    lane = lax.broadcasted_iota(jnp.int32, (1, LANES), 1)
    lo = lane < HEAD_DIM

    def expand_k(kfull):
        rolled = pltpu.roll(kfull, HEAD_DIM, 1)
        zero = jnp.zeros_like(kfull)
        return (jnp.where(lo, kfull, zero), jnp.where(lo, zero, rolled),
                jnp.where(lo, rolled, zero), jnp.where(lo, zero, kfull))

    def store_expanded(r0, nrows, kfull, vfull):
        for j, (kv, vv) in enumerate(zip(expand_k(kfull), expand_k(vfull))):
            kx_ref[j, r0 // 2:(r0 + nrows) // 2, :] = _u32(kv.astype(BF16))
            vx_ref[j, r0 // 2:(r0 + nrows) // 2, 0:LANES] = _u32(vv.astype(BF16))

    @pl.when(t == 0)
    def _():
        ones_lo = jnp.broadcast_to(jnp.where(lo, 1.0, 0.0), (rows_x, LANES)).astype(BF16)
        ones_hi = jnp.broadcast_to(jnp.where(lo, 0.0, 1.0), (rows_x, LANES)).astype(BF16)
        for j in range(4):
            vx_ref[j, :, LANES:2 * LANES] = _u32(ones_lo if j % 2 == 0 else ones_hi)
            kx_ref[j, data // 2:rows_x // 2, :] = jnp.zeros(((rows_x - data) // 2, LANES), jnp.uint32)
            vx_ref[j, data // 2:rows_x // 2, 0:LANES] = jnp.zeros(((rows_x - data) // 2, LANES), jnp.uint32)
        kf_ref[0:WINDOW, :] = pk_ref[0]
        vf_ref[0:WINDOW, :] = pv_ref[0]
        store_expanded(0, WINDOW, pk_ref[0], pv_ref[0])

    x1 = x1_ref[0]
    ms = jnp.mean(x1 * x1, axis=-1, keepdims=True)
    xhat = x1 * lax.rsqrt(ms + EPS)
    xk = (xhat * kvg_ref[...]).astype(BF16)
    xq = (xhat * bg_ref[...]).astype(BF16)

    kv = jnp.dot(xk, wkv_ref[...], preferred_element_type=F32)
    k_raw = kv[:, 0:LANES]
    v_new = kv[:, LANES:2 * LANES]
    k_ms = jnp.dot((k_raw * k_raw).astype(BF16), e128_ref[...], preferred_element_type=F32)
    k_new = k_raw * lax.rsqrt(k_ms + EPS) * kn_ref[...]
    kf_ref[WINDOW:WINDOW + tile_t, :] = k_new
    vf_ref[WINDOW:WINDOW + tile_t, :] = v_new
    store_expanded(WINDOW, tile_t, k_new, v_new)

    u = jnp.dot(xq, win_ref[...], preferred_element_type=F32)
    q_raw = u[:, :d]
    gate = u[:, d:]
    q_sq = (q_raw * q_raw).astype(BF16)
    seg = e256_ref.shape[0]
    q_ms = jnp.concatenate(
        [jnp.dot(q_sq[:, j * seg:(j + 1) * seg], e256_ref[...], preferred_element_type=F32)
         for j in range(d // seg)], axis=1)
    q = (q_raw * lax.rsqrt(q_ms + EPS) * qn_ref[...] * (HEAD_DIM ** -0.5 * LOG2E)).astype(BF16)

    col = lax.broadcasted_iota(jnp.int32, (1, 2 * KEY_SLOTS), 1) % KEY_SLOTS
    for c in range(n_c):
        r0 = c * chunk
        for g in range(N_KV_HEADS):
            band = slice(r0 // 2, (r0 + n_keys) // 2)
            tail = slice(data // 2, rows_x // 2)
            k_bd = _bf16(jnp.concatenate([kx_ref[2 * g, band, :], kx_ref[2 * g, tail, :],
                                          kx_ref[2 * g + 1, band, :], kx_ref[2 * g + 1, tail, :]], axis=0))
            v_bd = _bf16(jnp.concatenate([vx_ref[2 * g, band, :], vx_ref[2 * g, tail, :],
                                          vx_ref[2 * g + 1, band, :], vx_ref[2 * g + 1, tail, :]], axis=0))
            q_g = jnp.concatenate(
                [q[r0:r0 + chunk, (g * PAIRS + p) * LANES:(g * PAIRS + p + 1) * LANES]
                 for p in range(PAIRS)], axis=0)
            s = lax.dot_general(q_g, k_bd, (((1,), (1,)), ((), ())),
                                preferred_element_type=F32)
            s = s + bias_ref[g]
            if not has_past and r0 < WINDOW:
                n_dead = jnp.where(t == 0, WINDOW - r0, 0)
                s = s + jnp.where(col < n_dead, NEG, 0.0)
            p_blocks = []
            for p in range(PAIRS):
                row_p = []
                for e in range(2):
                    sb = s[p * chunk:(p + 1) * chunk, e * KEY_SLOTS:(e + 1) * KEY_SLOTS]
                    m = jnp.max(sb, axis=-1, keepdims=True)
                    row_p.append(jnp.exp2(sb - m).astype(BF16))
                p_blocks.append(jnp.concatenate(row_p, axis=1))
            prob = jnp.concatenate(p_blocks, axis=0)
            ov = jnp.dot(prob, v_bd, preferred_element_type=F32)
            o = ov[:, 0:LANES] / ov[:, LANES:2 * LANES]
            for p in range(PAIRS):
                c0 = (g * PAIRS + p) * LANES
                o_ref[r0:r0 + chunk, c0:c0 + LANES] = o[p * chunk:(p + 1) * chunk, :]

    og = (o_ref[...] * (gate * _sigmoid(gate))).astype(BF16)
    y_ref[0] = x1 + jnp.dot(og, wout_ref[...], preferred_element_type=F32)

    @pl.when(t == n_t - 1)
    def _():
        ko_ref[0] = kf_ref[tile_t:tile_t + WINDOW, :]
        vo_ref[0] = vf_ref[tile_t:tile_t + WINDOW, :]

    if n_t > 1:
        kf_ref[0:WINDOW, :] = kf_ref[tile_t:tile_t + WINDOW, :]
        vf_ref[0:WINDOW, :] = vf_ref[tile_t:tile_t + WINDOW, :]
        for j in range(4):
            kx_ref[j, 0:WINDOW // 2, :] = kx_ref[j, tile_t // 2:(tile_t + WINDOW) // 2, :]
            vx_ref[j, 0:WINDOW // 2, 0:LANES] = vx_ref[j, tile_t // 2:(tile_t + WINDOW) // 2, 0:LANES]


def _score_bias(chunk, sinks):
    shape = (N_KV_HEADS, PAIRS, 1, 2, 1)
    slopes = (2.0 ** (-8.0 * jnp.arange(1, N_HEADS + 1, dtype=F32) / N_HEADS)).reshape(shape)
    qi = jnp.arange(chunk, dtype=F32).reshape(1, 1, chunk, 1, 1)
    sj = jnp.arange(KEY_SLOTS, dtype=F32).reshape(1, 1, 1, 1, KEY_SLOTS)
    n_keys = WINDOW + chunk
    alibi = -(slopes * jnp.abs(qi + WINDOW - sj))
    bias = jnp.where(sj < n_keys, alibi, jnp.where(sj == n_keys, sinks.astype(F32).reshape(shape), NEG))
    bias = jnp.where(sj <= n_keys, bias * LOG2E, NEG)
    return bias.reshape(N_KV_HEADS, PAIRS * chunk, 2 * KEY_SLOTS)


def _head_mean_matrix(n):
    idx = jnp.arange(n) // HEAD_DIM
    return jnp.where(idx[:, None] == idx[None, :], 1.0 / HEAD_DIM, 0.0).astype(BF16)


def _b_layer(x1, past_k, past_v, kv_g, b_g, w_kv, k_norm, w_in, q_norm, sinks, w_out,
             *, tile_t, chunk, has_past):
    bsz, seq, d = x1.shape
    kvw = N_KV_HEADS * HEAD_DIM
    assert kvw == LANES and d == N_HEADS * HEAD_DIM
    assert seq % tile_t == 0 and tile_t % chunk == 0 and WINDOW + chunk < KEY_SLOTS
    n_t = seq // tile_t
    assert n_t == 1 or tile_t >= WINDOW
    rows_x = WINDOW + tile_t + (KEY_SLOTS - WINDOW - chunk)
    kern = functools.partial(_b_layer_kernel, tile_t=tile_t, chunk=chunk, d=d, n_t=n_t,
                             has_past=has_past)
    full = lambda shape: pl.BlockSpec(shape, lambda b, t: (0,) * len(shape))
    seg = 2 * LANES
    return pl.pallas_call(
        kern,
        grid=(bsz, n_t),
        in_specs=[
            pl.BlockSpec((1, tile_t, d), lambda b, t: (b, t, 0)),
            pl.BlockSpec((1, WINDOW, kvw), lambda b, t: (b, 0, 0)),
            pl.BlockSpec((1, WINDOW, kvw), lambda b, t: (b, 0, 0)),
            full((1, d)),
            full((1, d)),
            full((d, 2 * kvw)),
            full((1, kvw)),
            full((kvw, kvw)),
            full((d, 2 * d)),
            full((1, d)),
            full((seg, seg)),
            full((N_KV_HEADS, PAIRS * chunk, 2 * KEY_SLOTS)),
            full((d, d)),
        ],
        out_specs=[
            pl.BlockSpec((1, tile_t, d), lambda b, t: (b, t, 0)),
            pl.BlockSpec((1, WINDOW, kvw), lambda b, t: (b, 0, 0)),
            pl.BlockSpec((1, WINDOW, kvw), lambda b, t: (b, 0, 0)),
        ],
        out_shape=[
            jax.ShapeDtypeStruct((bsz, seq, d), F32),
            jax.ShapeDtypeStruct((bsz, WINDOW, kvw), F32),
            jax.ShapeDtypeStruct((bsz, WINDOW, kvw), F32),
        ],
        scratch_shapes=[
            pltpu.VMEM((WINDOW + tile_t, kvw), F32),
            pltpu.VMEM((WINDOW + tile_t, kvw), F32),
            pltpu.VMEM((4, rows_x // 2, LANES), jnp.uint32),
            pltpu.VMEM((4, rows_x // 2, 2 * LANES), jnp.uint32),
            pltpu.VMEM((tile_t, d), F32),
        ],
        compiler_params=pltpu.CompilerParams(
            dimension_semantics=("arbitrary", "arbitrary"),
            vmem_limit_bytes=VMEM_LIMIT_BYTES),
        name="b_layer",
    )(x1, past_k, past_v, kv_g.reshape(1, d), b_g.reshape(1, d), w_kv,
      jnp.tile(k_norm, N_KV_HEADS).reshape(1, kvw), _head_mean_matrix(kvw), w_in,
      jnp.tile(q_norm, N_HEADS).reshape(1, d), _head_mean_matrix(seg), _score_bias(chunk, sinks), w_out)


def kernel(x_prompt, x_sample, cache_k, cache_v, state_conv, state_rglru, a_norm, a_w_in, a_conv_w,
           a_conv_b, a_gate_a_w, a_gate_a_b, a_gate_x_w, a_gate_x_b, a_lambda, a_w_out, kv_norm, w_kv,
           k_norm, b_norm, b_w_in, q_norm, sinks, b_w_out):
    assert a_norm.shape[0] == 1 and b_norm.shape[0] == 1
    d = x_prompt.shape[-1]
    kvw = N_KV_HEADS * HEAD_DIM
    taps = CONV_WIDTH - 1
    a_w = (a_norm[0], a_w_in[0].astype(BF16), a_conv_w[0], a_conv_b[0], a_gate_a_w[0].astype(BF16),
           a_gate_a_b[0], a_gate_x_w[0].astype(BF16), a_gate_x_b[0], a_lambda[0],
           a_w_out[0].astype(BF16))
    b_w = (kv_norm, b_norm[0], w_kv.astype(BF16), k_norm, b_w_in[0].astype(BF16), q_norm[0], sinks[0],
           b_w_out[0].astype(BF16))

    def b_group(x1, past_k, past_v, has_past, tile_t, chunk):
        bsz = x1.shape[0]
        y, k_buf, v_buf = _b_layer(x1, past_k.reshape(bsz, WINDOW, kvw), past_v.reshape(bsz, WINDOW, kvw),
                                   *b_w, tile_t=tile_t, chunk=chunk, has_past=has_past)
        shape_kv = (bsz, WINDOW, N_KV_HEADS, HEAD_DIM)
        return y, k_buf.reshape(shape_kv), v_buf.reshape(shape_kv)

    bp = x_prompt.shape[0]
    x1_p, tail_p, hl_p = _a_layer(x_prompt, jnp.zeros((bp, taps * SUBLANES, d), F32),
                                  jnp.zeros((bp, SUBLANES, d), F32), *a_w, rows=256, chained=True,
                                  seqs_per_step=2)
    p_conv = tail_p[:, SUBLANES - 1::SUBLANES, :][None]
    p_h = hl_p[:, 0, :][None]
    zeros_kv = jnp.zeros((bp, WINDOW, kvw), F32)
    y_p, p_k, p_v = b_group(x1_p, zeros_kv, zeros_kv, False, 512, CHUNK)

    bs, ts, _ = x_sample.shape
    assert bs == SUBLANES
    hist_s = jnp.transpose(state_conv[0], (1, 0, 2)).reshape(1, taps * bs, d)
    x1_s, tail_s, hl_s = _a_layer(x_sample.reshape(1, bs * ts, d), hist_s, state_rglru, *a_w,
                                  rows=bs * ts, chained=False, seqs_per_step=1)
    s_conv = jnp.transpose(tail_s.reshape(taps, bs, d), (1, 0, 2))[None]
    s_h = hl_s
    y_s, s_k, s_v = b_group(x1_s.reshape(bs, ts, d), cache_k, cache_v, True, ts, ts)
    return (y_p, y_s, p_k, p_v, p_conv, p_h, s_k, s_v, s_conv, s_h)
```

```python
import functools

import jax
import jax.numpy as jnp
from jax import lax
from jax.experimental import pallas as pl
from jax.experimental.pallas import tpu as pltpu

F32 = jnp.float32
BF16 = jnp.bfloat16

EPS = 1e-6
CONV_WIDTH = 4
N_RG_BLOCKS = 4
RG_C = 8.0
N_HEADS = 16
HEAD_DIM = 64
N_KV_HEADS = 2
GROUP = N_HEADS // N_KV_HEADS
WINDOW = 128
CHUNK = 64

SUBLANES = 8
LANES = 128
VMEM_LIMIT_BYTES = 56 * 1024 * 1024

PAIRS = GROUP // 2
KEY_SLOTS = 2 * LANES
NEG = -1e30
LOG2E = 1.4426950408889634
A_SEQ_LAG = 2600


def _sigmoid(x):
    return jax.nn.sigmoid(x)


def _bf16(packed):
    return pltpu.bitcast(packed, BF16)


def _u32(x_bf16):
    return pltpu.bitcast(x_bf16, jnp.uint32)


def _segment_permutation(rows):
    steps = rows // SUBLANES
    r = jnp.arange(rows)
    src = (r % SUBLANES) * steps + r // SUBLANES
    return (src[:, None] == jnp.arange(rows)[None, :]).astype(BF16)


def _interleave(gens, lag):
    clock = {k: k * lag for k in range(len(gens))}
    while clock:
        k = min(clock, key=lambda j: (clock[j], j))
        try:
            clock[k] += next(gens[k])
        except StopIteration:
            del clock[k]


def _a_layer_kernel(x_ref, hist_ref, h0_ref, pm_ref, pmt_ref, g_ref, win_ref, cw_ref, cb_ref,
                    wa_ref, ba_ref, wx_ref, bx_ref, lam_ref, wout_ref,
                    x1_ref, tail_ref, hlast_ref,
                    xbe_ref, prev_ref, hc_ref, *, rows, d, chained):
    gens = [_a_layer_tile(x_ref.at[n], hist_ref.at[n], h0_ref.at[n], pm_ref, pmt_ref, g_ref, win_ref,
                          cw_ref, cb_ref, wa_ref, ba_ref, wx_ref, bx_ref, lam_ref, wout_ref,
                          x1_ref.at[n], tail_ref.at[n], hlast_ref.at[n],
                          xbe_ref.at[n], prev_ref.at[n], hc_ref.at[n], rows=rows, d=d, chained=chained)
            for n in range(x_ref.shape[0])]
    _interleave(gens, lag=A_SEQ_LAG)


def _a_layer_tile(x_ref, hist_ref, h0_ref, pm_ref, pmt_ref, g_ref, win_ref, cw_ref, cb_ref,
                  wa_ref, ba_ref, wx_ref, bx_ref, lam_ref, wout_ref,
                  x1_ref, tail_ref, hlast_ref,
                  xbe_ref, prev_ref, hc_ref, *, rows, d, chained):
    t = pl.program_id(1)
    steps = rows // SUBLANES
    taps = CONV_WIDTH - 1
    hdr = taps * SUBLANES
    blk = d // N_RG_BLOCKS
    sub = lax.broadcasted_iota(jnp.int32, (SUBLANES, d), 0)

    if chained:
        @pl.when(t == 0)
        def _():
            prev_ref[...] = hist_ref[...]
            hc_ref[...] = h0_ref[...]

    x = x_ref[...]
    ms = jnp.mean(x * x, axis=-1, keepdims=True)
    xn = (x * lax.rsqrt(ms + EPS) * g_ref[...]).astype(BF16)
    xn = jnp.dot(pm_ref[...], xn, preferred_element_type=F32).astype(BF16)
    yield 600

    for n in range(N_RG_BLOCKS):
        xbe_ref[hdr:hdr + rows, n * blk:(n + 1) * blk] = jnp.dot(
            xn, win_ref[:, n * blk:(n + 1) * blk], preferred_element_type=F32)
        yield 256
    gate = []
    for n in range(N_RG_BLOCKS):
        gate.append(jnp.dot(xn, win_ref[:, d + n * blk:d + (n + 1) * blk],
                            preferred_element_type=F32))
        yield 256

    if chained:
        for i in range(taps):
            r0 = hdr + (steps - taps + i) * SUBLANES
            before = pltpu.roll(prev_ref[i * SUBLANES:(i + 1) * SUBLANES, :], 1, 0)
            within = pltpu.roll(xbe_ref[r0:r0 + SUBLANES, :], 1, 0)
            xbe_ref[i * SUBLANES:(i + 1) * SUBLANES, :] = jnp.where(sub == 0, before, within)
        prev_ref[...] = xbe_ref[rows:rows + hdr, :]
    else:
        xbe_ref[0:hdr, :] = hist_ref[...]
    tail_ref[...] = xbe_ref[rows:rows + hdr, :]

    log_decay = RG_C * jax.nn.log_sigmoid(lam_ref[...])
    a_blk, b_blk = [], []
    for n in range(N_RG_BLOCKS):
        cs = slice(n * blk, (n + 1) * blk)
        xc = cb_ref[:, cs]
        for k in range(CONV_WIDTH):
            off = k * SUBLANES
            xc = xc + xbe_ref[off:off + rows, cs] * cw_ref[k:k + 1, cs]
        xcb = xc.astype(BF16)
        r = _sigmoid(jnp.dot(xcb, wa_ref[n], preferred_element_type=F32) + ba_ref[:, cs])
        i = _sigmoid(jnp.dot(xcb, wx_ref[n], preferred_element_type=F32) + bx_ref[:, cs])
        log_a = r * log_decay[:, cs]
        a = jnp.exp(log_a)
        one_minus_a2 = -jnp.tanh(log_a) * (a * a + 1.0)
        root = jnp.where(one_minus_a2 > 0.0, one_minus_a2 * lax.rsqrt(one_minus_a2), 0.0)
        a_blk.append(a)
        b_blk.append(root * (i * xc))
        yield 550
    a = jnp.concatenate(a_blk, axis=1)
    b = jnp.concatenate(b_blk, axis=1)

    h = b[0:SUBLANES, :]
    c = a[0:SUBLANES, :]
    h_loc, a_cum = [h], [c]
    for p in range(1, steps):
        a_p = a[p * SUBLANES:(p + 1) * SUBLANES, :]
        h = a_p * h + b[p * SUBLANES:(p + 1) * SUBLANES, :]
        c = c * a_p
        h_loc.append(h)
        a_cum.append(c)

    if chained:
        a_seg, b_seg = c, h
        for dist in (1, 2, 4):
            a_prev = pltpu.roll(a_seg, dist, 0)
            b_prev = pltpu.roll(b_seg, dist, 0)
            m = sub >= dist
            b_seg = jnp.where(m, a_seg * b_prev + b_seg, b_seg)
            a_seg = jnp.where(m, a_seg * a_prev, a_seg)
        after = a_seg * hc_ref[...] + b_seg
        h_in = jnp.where(sub == 0, hc_ref[...], pltpu.roll(after, 1, 0))
        h_end = jnp.broadcast_to(after[SUBLANES - 1:SUBLANES, :], (SUBLANES, d))
        hc_ref[...] = h_end
    else:
        h_in = h0_ref[...]
    h_all = [h_loc[p] + a_cum[p] * h_in for p in range(steps)]
    hlast_ref[...] = h_end if chained else h_all[steps - 1]
    yield 450

    gate = jnp.concatenate(gate, axis=1)
    hg = (jnp.concatenate(h_all, axis=0) * (gate * _sigmoid(gate))).astype(BF16)
    hg = jnp.dot(pmt_ref[...], hg, preferred_element_type=F32).astype(BF16)
    yield 450
    for n in range(N_RG_BLOCKS):
        cs = slice(n * blk, (n + 1) * blk)
        x1_ref[:, cs] = x[:, cs] + jnp.dot(hg, wout_ref[:, cs], preferred_element_type=F32)
        yield 256


def _a_layer(x, hist, h0, g, w_in, conv_w, conv_b, wa, ba, wx, bx, lam, w_out, *, rows, chained,
             seqs_per_step):
    n_seq, n_rows, d = x.shape
    taps = CONV_WIDTH - 1
    assert n_rows % rows == 0 and rows % SUBLANES == 0 and rows // SUBLANES >= taps
    assert n_seq % seqs_per_step == 0
    n_t = n_rows // rows
    ns = seqs_per_step
    assert chained or n_t == 1
    kern = functools.partial(_a_layer_kernel, rows=rows, d=d, chained=chained)
    row = lambda a: a.reshape(1, d)
    full = lambda shape: pl.BlockSpec(shape, lambda b, t: (0,) * len(shape))
    perm = _segment_permutation(rows)
    nb = N_RG_BLOCKS
    return pl.pallas_call(
        kern,
        grid=(n_seq // ns, n_t),
        in_specs=[
            pl.BlockSpec((ns, rows, d), lambda b, t: (b, t, 0)),
            pl.BlockSpec((ns, taps * SUBLANES, d), lambda b, t: (b, 0, 0)),
            pl.BlockSpec((ns, SUBLANES, d), lambda b, t: (b, 0, 0)),
            full((rows, rows)),
            full((rows, rows)),
            full((1, d)),
            full((d, 2 * d)),
            full((CONV_WIDTH, d)),
            full((1, d)),
            full((nb, d // nb, d // nb)),
            full((1, d)),
            full((nb, d // nb, d // nb)),
            full((1, d)),
            full((1, d)),
            full((d, d)),
        ],
        out_specs=[
            pl.BlockSpec((ns, rows, d), lambda b, t: (b, t, 0)),
            pl.BlockSpec((ns, taps * SUBLANES, d), lambda b, t: (b, 0, 0)),
            pl.BlockSpec((ns, SUBLANES, d), lambda b, t: (b, 0, 0)),
        ],
        out_shape=[
            jax.ShapeDtypeStruct((n_seq, n_rows, d), F32),
            jax.ShapeDtypeStruct((n_seq, taps * SUBLANES, d), F32),
            jax.ShapeDtypeStruct((n_seq, SUBLANES, d), F32),
        ],
        scratch_shapes=[
            pltpu.VMEM((ns, taps * SUBLANES + rows, d), F32),
            pltpu.VMEM((ns, taps * SUBLANES, d), F32),
            pltpu.VMEM((ns, SUBLANES, d), F32),
        ],
        compiler_params=pltpu.CompilerParams(
            dimension_semantics=("arbitrary", "arbitrary"),
            vmem_limit_bytes=VMEM_LIMIT_BYTES),
        name="a_layer",
    )(x, hist, h0, perm, perm.T, row(g), w_in, conv_w, row(conv_b), wa, row(ba), wx, row(bx),
      row(lam), w_out)


def _b_layer_kernel(x1_ref, pk_ref, pv_ref, kvg_ref, bg_ref, wkv_ref, kn_ref, e128_ref,
                    win_ref, qn_ref, e256_ref, bias_ref, wout_ref,
                    y_ref, ko_ref, vo_ref,
                    kf_ref, vf_ref, kx_ref, vx_ref, o_ref,
                    *, tile_t, chunk, d, n_t, has_past):
    t = pl.program_id(1)
    ns = x1_ref.shape[0]
    n_c = tile_t // chunk
    packed = kx_ref.dtype == jnp.uint32
    rpw = 2 if packed else 1
    rows_x = rpw * kx_ref.shape[1] // ns
    to_scratch = _u32 if packed else (lambda v: v)
    from_scratch = _bf16 if packed else (lambda v: v)

    def span(n, r0, nrows):
        return slice((n * rows_x + r0) // rpw, (n * rows_x + r0 + nrows) // rpw)
    n_keys = WINDOW + chunk
    data = WINDOW + tile_t
    lane = lax.broadcasted_iota(jnp.int32, (1, LANES), 1)
    lo = lane < HEAD_DIM

    def expand_k(kfull):
        rolled = pltpu.roll(kfull, HEAD_DIM, 1)
        zero = jnp.zeros_like(kfull)
        return (jnp.where(lo, kfull, zero), jnp.where(lo, zero, rolled),
                jnp.where(lo, rolled, zero), jnp.where(lo, zero, kfull))

    def store_expanded(n, r0, nrows, kfull, vfull):
        for j, (kv, vv) in enumerate(zip(expand_k(kfull), expand_k(vfull))):
            kx_ref[j, span(n, r0, nrows), :] = to_scratch(kv.astype(BF16))
            vx_ref[j, span(n, r0, nrows), 0:LANES] = to_scratch(vv.astype(BF16))

    @pl.when(t == 0)
    def _():
        ones_lo = jnp.broadcast_to(jnp.where(lo, 1.0, 0.0), (rows_x, LANES)).astype(BF16)
        ones_hi = jnp.broadcast_to(jnp.where(lo, 0.0, 1.0), (rows_x, LANES)).astype(BF16)
        zeros = jnp.zeros(((rows_x - data) // rpw, LANES), kx_ref.dtype)
        for n in range(ns):
            for j in range(4):
                vx_ref[j, span(n, 0, rows_x), LANES:2 * LANES] = to_scratch(ones_lo if j % 2 == 0 else ones_hi)
                kx_ref[j, span(n, data, rows_x - data), :] = zeros
                vx_ref[j, span(n, data, rows_x - data), 0:LANES] = zeros
            kf_ref[n, 0:WINDOW, :] = pk_ref[n]
            vf_ref[n, 0:WINDOW, :] = pv_ref[n]
            store_expanded(n, 0, WINDOW, pk_ref[n], pv_ref[n])

    x1 = jnp.concatenate([x1_ref[n] for n in range(ns)], axis=0)
    ms = jnp.mean(x1 * x1, axis=-1, keepdims=True)
    xhat = x1 * lax.rsqrt(ms + EPS)
    xk = (xhat * kvg_ref[...]).astype(BF16)
    xq = (xhat * bg_ref[...]).astype(BF16)

    kv = jnp.dot(xk, wkv_ref[...], preferred_element_type=F32)
    k_raw = kv[:, 0:LANES]
    v_new = kv[:, LANES:2 * LANES]
    k_ms = jnp.dot((k_raw * k_raw).astype(BF16), e128_ref[...], preferred_element_type=F32)
    k_new = k_raw * lax.rsqrt(k_ms + EPS) * kn_ref[...]
    for n in range(ns):
        rows = slice(n * tile_t, (n + 1) * tile_t)
        kf_ref[n, WINDOW:WINDOW + tile_t, :] = k_new[rows]
        vf_ref[n, WINDOW:WINDOW + tile_t, :] = v_new[rows]
        store_expanded(n, WINDOW, tile_t, kf_ref[n, WINDOW:WINDOW + tile_t, :],
                       vf_ref[n, WINDOW:WINDOW + tile_t, :])

    u = jnp.dot(xq, win_ref[...], preferred_element_type=F32)
    q_raw = u[:, :d]
    gate = u[:, d:]
    q_sq = (q_raw * q_raw).astype(BF16)
    seg = e256_ref.shape[0]
    q_ms = jnp.concatenate(
        [jnp.dot(q_sq[:, j * seg:(j + 1) * seg], e256_ref[...], preferred_element_type=F32)
         for j in range(d // seg)], axis=1)
    q = q_raw * lax.rsqrt(q_ms + EPS) * qn_ref[...] * (HEAD_DIM ** -0.5 * LOG2E)

    col = lax.broadcasted_iota(jnp.int32, (1, 2 * KEY_SLOTS), 1) % KEY_SLOTS
    for n, c, g in [(n, c, g) for n in range(ns) for c in range(n_c) for g in range(N_KV_HEADS)]:
        r0 = c * chunk
        q0 = n * tile_t + r0
        band = span(n, r0, n_keys)
        tail = span(n, data, rows_x - data)
        k_bd = from_scratch(jnp.concatenate([kx_ref[2 * g, band, :], kx_ref[2 * g, tail, :],
                                             kx_ref[2 * g + 1, band, :], kx_ref[2 * g + 1, tail, :]], axis=0))
        v_bd = from_scratch(jnp.concatenate([vx_ref[2 * g, band, :], vx_ref[2 * g, tail, :],
                                             vx_ref[2 * g + 1, band, :], vx_ref[2 * g + 1, tail, :]], axis=0))
        q_g = jnp.concatenate(
            [q[q0:q0 + chunk, (g * PAIRS + p) * LANES:(g * PAIRS + p + 1) * LANES]
             for p in range(PAIRS)], axis=0).astype(BF16)
        s = lax.dot_general(q_g, k_bd, (((1,), (1,)), ((), ())),
                            preferred_element_type=F32)
        s = s + bias_ref[g]
        if not has_past and r0 < WINDOW:
            n_dead = jnp.where(t == 0, WINDOW - r0, 0)
            s = s + jnp.where(col < n_dead, NEG, 0.0)
        p_blocks = []
        for p in range(PAIRS):
            row_p = []
            for e in range(2):
                sb = s[p * chunk:(p + 1) * chunk, e * KEY_SLOTS:(e + 1) * KEY_SLOTS]
                m = jnp.max(sb, axis=-1, keepdims=True)
                row_p.append(jnp.exp2(sb - m).astype(BF16))
            p_blocks.append(jnp.concatenate(row_p, axis=1))
        prob = jnp.concatenate(p_blocks, axis=0)
        ov = jnp.dot(prob, v_bd, preferred_element_type=F32)
        o = ov[:, 0:LANES] / ov[:, LANES:2 * LANES]
        for p in range(PAIRS):
            c0 = (g * PAIRS + p) * LANES
            o_ref[q0:q0 + chunk, c0:c0 + LANES] = o[p * chunk:(p + 1) * chunk, :]

    og = (o_ref[...] * (gate * _sigmoid(gate))).astype(BF16)
    y = x1 + jnp.dot(og, wout_ref[...], preferred_element_type=F32)
    for n in range(ns):
        y_ref[n] = y[n * tile_t:(n + 1) * tile_t, :]

    @pl.when(t == n_t - 1)
    def _():
        for n in range(ns):
            ko_ref[n] = kf_ref[n, tile_t:tile_t + WINDOW, :]
            vo_ref[n] = vf_ref[n, tile_t:tile_t + WINDOW, :]

    if n_t > 1:
        for n in range(ns):
            kf_ref[n, 0:WINDOW, :] = kf_ref[n, tile_t:tile_t + WINDOW, :]
            vf_ref[n, 0:WINDOW, :] = vf_ref[n, tile_t:tile_t + WINDOW, :]
            for j in range(4):
                kx_ref[j, span(n, 0, WINDOW), :] = kx_ref[j, span(n, tile_t, WINDOW), :]
                vx_ref[j, span(n, 0, WINDOW), 0:LANES] = vx_ref[j, span(n, tile_t, WINDOW), 0:LANES]


def _score_bias(chunk, sinks):
    shape = (N_KV_HEADS, PAIRS, 1, 2, 1)
    slopes = (2.0 ** (-8.0 * jnp.arange(1, N_HEADS + 1, dtype=F32) / N_HEADS)).reshape(shape)
    qi = jnp.arange(chunk, dtype=F32).reshape(1, 1, chunk, 1, 1)
    sj = jnp.arange(KEY_SLOTS, dtype=F32).reshape(1, 1, 1, 1, KEY_SLOTS)
    n_keys = WINDOW + chunk
    alibi = -(slopes * jnp.abs(qi + WINDOW - sj))
    bias = jnp.where(sj < n_keys, alibi, jnp.where(sj == n_keys, sinks.astype(F32).reshape(shape), NEG))
    bias = jnp.where(sj <= n_keys, bias * LOG2E, NEG)
    return bias.reshape(N_KV_HEADS, PAIRS * chunk, 2 * KEY_SLOTS)


def _head_mean_matrix(n):
    idx = jnp.arange(n) // HEAD_DIM
    return jnp.where(idx[:, None] == idx[None, :], 1.0 / HEAD_DIM, 0.0).astype(BF16)


def _b_layer(x1, past_k, past_v, kv_g, b_g, w_kv, k_norm, w_in, q_norm, sinks, w_out,
             *, tile_t, chunk, has_past, seqs_per_step):
    bsz, seq, d = x1.shape
    ns = seqs_per_step
    assert bsz % ns == 0 and tile_t % (2 * SUBLANES) == 0
    kvw = N_KV_HEADS * HEAD_DIM
    assert kvw == LANES and d == N_HEADS * HEAD_DIM
    assert seq % tile_t == 0 and tile_t % chunk == 0 and WINDOW + chunk < KEY_SLOTS
    n_t = seq // tile_t
    assert n_t == 1 or tile_t >= WINDOW
    rows_x = WINDOW + tile_t + (KEY_SLOTS - WINDOW - chunk)
    kv_dtype, rpw = (jnp.uint32, 2) if ns == 1 else (BF16, 1)
    kern = functools.partial(_b_layer_kernel, tile_t=tile_t, chunk=chunk, d=d, n_t=n_t,
                             has_past=has_past)
    full = lambda shape: pl.BlockSpec(shape, lambda b, t: (0,) * len(shape))
    seg = 2 * LANES
    return pl.pallas_call(
        kern,
        grid=(bsz // ns, n_t),
        in_specs=[
            pl.BlockSpec((ns, tile_t, d), lambda b, t: (b, t, 0)),
            pl.BlockSpec((ns, WINDOW, kvw), lambda b, t: (b, 0, 0)),
            pl.BlockSpec((ns, WINDOW, kvw), lambda b, t: (b, 0, 0)),
            full((1, d)),
            full((1, d)),
            full((d, 2 * kvw)),
            full((1, kvw)),
            full((kvw, kvw)),
            full((d, 2 * d)),
            full((1, d)),
            full((seg, seg)),
            full((N_KV_HEADS, PAIRS * chunk, 2 * KEY_SLOTS)),
            full((d, d)),
        ],
        out_specs=[
            pl.BlockSpec((ns, tile_t, d), lambda b, t: (b, t, 0)),
            pl.BlockSpec((ns, WINDOW, kvw), lambda b, t: (b, 0, 0)),
            pl.BlockSpec((ns, WINDOW, kvw), lambda b, t: (b, 0, 0)),
        ],
        out_shape=[
            jax.ShapeDtypeStruct((bsz, seq, d), F32),
            jax.ShapeDtypeStruct((bsz, WINDOW, kvw), F32),
            jax.ShapeDtypeStruct((bsz, WINDOW, kvw), F32),
        ],
        scratch_shapes=[
            pltpu.VMEM((ns, WINDOW + tile_t, kvw), F32),
            pltpu.VMEM((ns, WINDOW + tile_t, kvw), F32),
            pltpu.VMEM((4, ns * rows_x // rpw, LANES), kv_dtype),
            pltpu.VMEM((4, ns * rows_x // rpw, 2 * LANES), kv_dtype),
            pltpu.VMEM((ns * tile_t, d), F32),
        ],
        compiler_params=pltpu.CompilerParams(
            dimension_semantics=("arbitrary", "arbitrary"),
            vmem_limit_bytes=VMEM_LIMIT_BYTES),
        name="b_layer",
    )(x1, past_k, past_v, kv_g.reshape(1, d), b_g.reshape(1, d), w_kv,
      jnp.tile(k_norm, N_KV_HEADS).reshape(1, kvw), _head_mean_matrix(kvw), w_in,
      jnp.tile(q_norm, N_HEADS).reshape(1, d), _head_mean_matrix(seg), _score_bias(chunk, sinks), w_out)


def kernel(x_prompt, x_sample, cache_k, cache_v, state_conv, state_rglru, a_norm, a_w_in, a_conv_w,
           a_conv_b, a_gate_a_w, a_gate_a_b, a_gate_x_w, a_gate_x_b, a_lambda, a_w_out, kv_norm, w_kv,
           k_norm, b_norm, b_w_in, q_norm, sinks, b_w_out):
    assert a_norm.shape[0] == 1 and b_norm.shape[0] == 1
    d = x_prompt.shape[-1]
    kvw = N_KV_HEADS * HEAD_DIM
    taps = CONV_WIDTH - 1
    a_w = (a_norm[0], a_w_in[0].astype(BF16), a_conv_w[0], a_conv_b[0], a_gate_a_w[0].astype(BF16),
           a_gate_a_b[0], a_gate_x_w[0].astype(BF16), a_gate_x_b[0], a_lambda[0],
           a_w_out[0].astype(BF16))
    b_w = (kv_norm, b_norm[0], w_kv.astype(BF16), k_norm, b_w_in[0].astype(BF16), q_norm[0], sinks[0],
           b_w_out[0].astype(BF16))

    def b_group(x1, past_k, past_v, has_past, tile_t, chunk, seqs_per_step):
        bsz = x1.shape[0]
        y, k_buf, v_buf = _b_layer(x1, past_k.reshape(bsz, WINDOW, kvw), past_v.reshape(bsz, WINDOW, kvw),
                                   *b_w, tile_t=tile_t, chunk=chunk, has_past=has_past,
                                   seqs_per_step=seqs_per_step)
        shape_kv = (bsz, WINDOW, N_KV_HEADS, HEAD_DIM)
        return y, k_buf.reshape(shape_kv), v_buf.reshape(shape_kv)

    bp = x_prompt.shape[0]
    x1_p, tail_p, hl_p = _a_layer(x_prompt, jnp.zeros((bp, taps * SUBLANES, d), F32),
                                  jnp.zeros((bp, SUBLANES, d), F32), *a_w, rows=256, chained=True,
                                  seqs_per_step=2)
    p_conv = tail_p[:, SUBLANES - 1::SUBLANES, :][None]
    p_h = hl_p[:, 0, :][None]
    zeros_kv = jnp.zeros((bp, WINDOW, kvw), F32)
    y_p, p_k, p_v = b_group(x1_p, zeros_kv, zeros_kv, False, 512, CHUNK, 1)

    bs, ts, _ = x_sample.shape
    assert bs == SUBLANES
    hist_s = jnp.transpose(state_conv[0], (1, 0, 2)).reshape(1, taps * bs, d)
    x1_s, tail_s, hl_s = _a_layer(x_sample.reshape(1, bs * ts, d), hist_s, state_rglru, *a_w,
                                  rows=bs * ts, chained=False, seqs_per_step=1)
    s_conv = jnp.transpose(tail_s.reshape(taps, bs, d), (1, 0, 2))[None]
    s_h = hl_s
    y_s, s_k, s_v = b_group(x1_s.reshape(bs, ts, d), cache_k, cache_v, True, ts, ts, bs)
    return (y_p, y_s, p_k, p_v, p_conv, p_h, s_k, s_v, s_conv, s_h)
```
